```python
import jax, jax.numpy as jnp
from jax import lax
import numpy as np

D_MODEL = 1024
BATCH = 2
SEQ = 16384
DEPTH = 2

POOL_WIDTH = D_MODEL // 2
N_POOL_GROUPS = 4
POOL_GROUP_DIM = POOL_WIDTH // N_POOL_GROUPS
POOL_WINDOWS = (2, 4, 8, 16)
HGRN_WIDTH = D_MODEL // 2
HGRN_EXPAND = 128
HGRN_HEADS = HGRN_WIDTH // HGRN_EXPAND
HGRN_HK = HGRN_HEADS * HGRN_EXPAND
HGRN_HEAD_V = HGRN_WIDTH // HGRN_HEADS
CHUNK = 64
N_BRANCHES = 2
OFF_Q = POOL_WIDTH
OFF_F = OFF_Q + HGRN_HK
OFF_I = OFF_F + HGRN_HK
OFF_G = OFF_I + HGRN_WIDTH
OFF_GATES = OFF_G + HGRN_WIDTH
IN_COLS = OFF_GATES + N_BRANCHES * D_MODEL
D_FF_DENSE = 2816
N_EXPERTS = 8
TOP_K = 2
D_FF_EXPERT = 3584
N_DENSE = (DEPTH + 1) // 2
N_MOE = DEPTH // 2
EPS = 1e-6
LB_FLOOR = 1e-30

kernel_name = "hybrid_pool_hgrn2_gated_moe_trunk"


def _rmsnorm(x, w):
    xf = x.astype(jnp.float32)
    y = xf * lax.rsqrt(jnp.mean(xf * xf, axis=-1, keepdims=True) + EPS)
    return (y * w.astype(jnp.float32)).astype(x.dtype)


def _swiglu(h, w_gate, w_up, w_down):
    return (jax.nn.silu(h @ w_gate) * (h @ w_up)) @ w_down


def _pool_mixer(u, pool_w, pool_scale):
    B, S, _ = u.shape
    ug = u.reshape(B, S, N_POOL_GROUPS, POOL_GROUP_DIM).astype(jnp.float32)
    csum = jnp.cumsum(ug, axis=1)
    pos = jnp.arange(1, S + 1, dtype=jnp.float32)[None, :, None]
    outs = []
    for gi, w in enumerate(POOL_WINDOWS):
        c = csum[:, :, gi]
        c_prev = jnp.pad(c, ((0, 0), (w, 0), (0, 0)))[:, :S]
        outs.append((c - c_prev) / jnp.minimum(pos, float(w)))
    pooled = (jnp.stack(outs, axis=2) - ug).astype(u.dtype)
    mixed = jnp.einsum('bsgc,gcd->bsgd', pooled, pool_w)
    return mixed.reshape(B, S, POOL_WIDTH) * pool_scale


def _hgrn2_chunked(q, k, v, log_f):
    B, S, H, dk = q.shape
    dv = v.shape[-1]
    n_chunks = S // CHUNK

    def to_chunks(t):
        return t.reshape(B, n_chunks, CHUNK, H, t.shape[-1]).transpose(1, 0, 3, 2, 4)

    qc, kc, vc = to_chunks(q), to_chunks(k), to_chunks(v)
    G = jnp.cumsum(to_chunks(log_f), axis=3)
    mask = jnp.tril(jnp.ones((CHUNK, CHUNK), dtype=bool))[:, :, None]

    def step(state, xs):
        qb, kb, vb, Gb = xs
        diff = Gb[:, :, :, None, :] - Gb[:, :, None, :, :]
        decay = jnp.where(mask, jnp.exp(jnp.where(mask, diff, 0.0)), 0.0)
        scores = jnp.einsum('bhtk,bhsk,bhtsk->bhts', qb, kb, decay)
        o = jnp.einsum('bhts,bhsv->bhtv', scores, vb) + jnp.einsum('bhtk,bhkv->bhtv', qb * jnp.exp(Gb), state)
        G_last = Gb[:, :, -1:, :]
        new_state = jnp.exp(G_last[:, :, 0, :])[..., None] * state + jnp.einsum(
            'bhsk,bhsv->bhkv', kb * jnp.exp(G_last - Gb), vb)
        return new_state, o

    state0 = jnp.zeros((B, H, dk, dv), jnp.float32)
    _, o = lax.scan(step, state0, (qc, kc, vc, G))
    return o.transpose(1, 0, 3, 2, 4).reshape(B, S, H, dv)


def _hgrn2_mixer(q_pre, f_pre, i_pre, g_pre, lb, norm_w):
    B, S, _ = q_pre.shape
    f32 = jnp.float32
    q = jax.nn.silu(q_pre.astype(f32)).reshape(B, S, HGRN_HEADS, HGRN_EXPAND)
    log_lb = jnp.log(jnp.maximum(lb, LB_FLOOR))
    log_f = jnp.logaddexp(log_lb, jnp.log1p(-lb) + jax.nn.log_sigmoid(f_pre.astype(f32)))
    k = -jnp.expm1(log_f)
    log_f = log_f.reshape(B, S, HGRN_HEADS, HGRN_EXPAND)
    k = k.reshape(B, S, HGRN_HEADS, HGRN_EXPAND)
    v = i_pre.astype(f32).reshape(B, S, HGRN_HEADS, HGRN_HEAD_V)
    o = _hgrn2_chunked(q, k, v, log_f)
    o = o * lax.rsqrt(jnp.mean(o * o, axis=-1, keepdims=True) + EPS) * norm_w.astype(f32)
    g = jax.nn.silu(g_pre.astype(f32)).reshape(B, S, HGRN_HEADS, HGRN_HEAD_V)
    return (o * g).reshape(B, S, HGRN_WIDTH).astype(q_pre.dtype)


def _token_mixer(h, w_in, gate_bias, pool_w, pool_scale, pool_up, lb, hgrn_norm_w, hgrn_up, w_out):
    proj = h @ w_in
    u = proj[..., :OFF_Q]
    q_pre = proj[..., OFF_Q:OFF_F]
    f_pre = proj[..., OFF_F:OFF_I]
    i_pre = proj[..., OFF_I:OFF_G]
    g_pre = proj[..., OFF_G:OFF_GATES]
    gates = jax.nn.sigmoid(proj[..., OFF_GATES:] + gate_bias)
    y_pool = _pool_mixer(u, pool_w, pool_scale) @ pool_up
    y_hgrn = _hgrn2_mixer(q_pre, f_pre, i_pre, g_pre, lb, hgrn_norm_w) @ hgrn_up
    merged = gates[..., :D_MODEL] * y_pool + gates[..., D_MODEL:] * y_hgrn
    return merged @ w_out


def _moe(h, router, w_gate, w_up, w_down):
    logits = (h @ router).astype(jnp.float32)
    top_v, top_i = lax.top_k(logits, TOP_K)
    top_w = jax.nn.softmax(top_v, axis=-1)
    combine = jnp.sum(jax.nn.one_hot(top_i, N_EXPERTS, dtype=jnp.float32) * top_w[..., None], axis=-2)
    y = jnp.zeros_like(h)
    for e in range(N_EXPERTS):
        y = y + combine[..., e:e + 1].astype(h.dtype) * _swiglu(h, w_gate[e], w_up[e], w_down[e])
    return y


def setup_inputs(seed: int = 0) -> dict:
    key = jax.random.key(seed)
    ks = jax.random.split(key, 24)
    f32 = jnp.float32

    def w(k, shape, fan_in):
        return jax.random.normal(k, shape, f32) * (fan_in ** -0.5)

    def gain(k, shape):
        return 1.0 + 0.05 * jax.random.normal(k, shape, f32)

    return {
        "x": jax.random.normal(ks[0], (BATCH, SEQ, D_MODEL), f32),
        "attn_norm_w": gain(ks[1], (DEPTH, D_MODEL)),
        "w_in": w(ks[2], (DEPTH, D_MODEL, IN_COLS), D_MODEL),
        "gate_bias": 0.01 * jax.random.normal(ks[3], (DEPTH, N_BRANCHES * D_MODEL), f32),
        "pool_w": w(ks[4], (DEPTH, N_POOL_GROUPS, POOL_GROUP_DIM, POOL_GROUP_DIM), POOL_GROUP_DIM),
        "pool_scale": gain(ks[5], (DEPTH, POOL_WIDTH)),
        "pool_up": w(ks[6], (DEPTH, POOL_WIDTH, D_MODEL), POOL_WIDTH),
        "hgrn_lower_bounds": 0.5 * jax.random.normal(ks[7], (DEPTH, HGRN_HK), f32),
        "hgrn_norm_w": gain(ks[8], (DEPTH, HGRN_HEAD_V)),
        "hgrn_up": w(ks[9], (DEPTH, HGRN_WIDTH, D_MODEL), HGRN_WIDTH),
        "w_out": w(ks[10], (DEPTH, D_MODEL, D_MODEL), D_MODEL),
        "ffn_norm_w": gain(ks[11], (DEPTH, D_MODEL)),
        "dense_w_gate": w(ks[12], (N_DENSE, D_MODEL, D_FF_DENSE), D_MODEL),
        "dense_w_up": w(ks[13], (N_DENSE, D_MODEL, D_FF_DENSE), D_MODEL),
        "dense_w_down": w(ks[14], (N_DENSE, D_FF_DENSE, D_MODEL), D_FF_DENSE),
        "moe_router": w(ks[15], (N_MOE, D_MODEL, N_EXPERTS), D_MODEL),
        "moe_w_gate": w(ks[16], (N_MOE, N_EXPERTS, D_MODEL, D_FF_EXPERT), D_MODEL),
        "moe_w_up": w(ks[17], (N_MOE, N_EXPERTS, D_MODEL, D_FF_EXPERT), D_MODEL),
        "moe_w_down": w(ks[18], (N_MOE, N_EXPERTS, D_FF_EXPERT, D_MODEL), D_FF_EXPERT),
        "final_norm_w": gain(ks[19], (D_MODEL,)),
    }


def reference(x, attn_norm_w, w_in, gate_bias, pool_w, pool_scale, pool_up, hgrn_lower_bounds,
              hgrn_norm_w, hgrn_up, w_out, ffn_norm_w, dense_w_gate, dense_w_up, dense_w_down,
              moe_router, moe_w_gate, moe_w_up, moe_w_down, final_norm_w):
    lb_soft = jax.nn.softmax(hgrn_lower_bounds.astype(jnp.float32), axis=0)
    lower_bounds = jnp.clip(jnp.cumsum(lb_soft, axis=0) - lb_soft[0:1], 0.0, 1.0 - 1e-6)
    for l in range(DEPTH):
        h = _rmsnorm(x, attn_norm_w[l])
        x = x + _token_mixer(h, w_in[l], gate_bias[l], pool_w[l], pool_scale[l], pool_up[l],
                             lower_bounds[l], hgrn_norm_w[l], hgrn_up[l], w_out[l])
        h = _rmsnorm(x, ffn_norm_w[l])
        if l % 2 == 0:
            j = l // 2
            x = x + _swiglu(h, dense_w_gate[j], dense_w_up[j], dense_w_down[j])
        else:
            j = l // 2
            x = x + _moe(h, moe_router[j], moe_w_gate[j], moe_w_up[j], moe_w_down[j])
    return _rmsnorm(x, final_norm_w)
```

```python
import functools

import jax
import jax.numpy as jnp
from jax import lax
from jax.experimental import pallas as pl
from jax.experimental.pallas import tpu as pltpu

F32 = jnp.float32
BF16 = jnp.bfloat16

EPS = 1e-6
LB_FLOOR = 1e-30
POOL_WINDOWS = (2, 4, 8, 16)
N_POOL_GROUPS = 4
HGRN_HEADS = 4
TOP_K = 2
LANE = 128

MIXER_TM = 256
POOL_CARRY = 16
DENSE_TM = 512
ROUTER_TM = 512
MOE_BM = 512
MOE_TF = 896
COMBINE_TM = 256
VMEM_LIMIT = 56 * 1024 * 1024


def _const_spec(shape):
    nd = len(shape)
    return pl.BlockSpec(shape, lambda *_: (0,) * nd, pipeline_mode=pl.Buffered(1))


def _dot(a, b):
    return jnp.dot(a, b, preferred_element_type=F32)


def _dot_nt(a, b):
    return lax.dot_general(a, b, (((1,), (1,)), ((), ())), preferred_element_type=F32)


def _dot_tn(a, b):
    return lax.dot_general(a, b, (((0,), (0,)), ((), ())), preferred_element_type=F32)


def _sigmoid(x):
    return 1.0 / (1.0 + jnp.exp(-x))


def _rmsnorm(x, w):
    ms = jnp.mean(x * x, axis=-1, keepdims=True)
    return x * lax.rsqrt(ms + EPS) * w


def _mixer_kernel(x_ref, nw_ref, win_ref, gb_ref, pw_ref, ps_ref, pup_ref,
                  lbf_ref, oml_ref, dl_ref, hnw_ref, hup_ref, wout_ref,
                  o_ref, carry_ref, state_ref):
    tm = x_ref.shape[1]
    d = x_ref.shape[2]
    pw_cols = pup_ref.shape[0]
    hw = hup_ref.shape[0]
    gd = pw_cols // N_POOL_GROUPS
    hd = hw // HGRN_HEADS
    off_q = pw_cols
    off_f = off_q + hw
    off_i = off_f + hw
    off_g = off_i + hw
    off_gates = off_g + hw
    si = pl.program_id(1)

    @pl.when(si == 0)
    def _():
        carry_ref[...] = jnp.zeros_like(carry_ref)
        state_ref[...] = jnp.zeros_like(state_ref)

    x = x_ref[0]
    h = _rmsnorm(x, nw_ref[...]).astype(BF16)

    u = _dot(h, win_ref[:, 0:pw_cols])
    ext = jnp.concatenate([carry_ref[...], u], axis=0)
    carry_ref[...] = u[tm - POOL_CARRY:, :]
    pos =(si * tm + 1 + lax.broadcasted_iota(jnp.int32, (tm, 1), 0)).astype(F32)
    acc = ext
    span = 1
    mixed = []
    for gi, w in enumerate(POOL_WINDOWS):
        while span < w:
            acc = acc + pltpu.roll(acc, span, 0)
            span *= 2
        wsum = acc[POOL_CARRY:, 0:gd]
        acc = acc[:, gd:] if gi + 1 < N_POOL_GROUPS else None
        ug = u[:, gi * gd:(gi + 1) * gd]
        pooled = wsum * (1.0 / jnp.minimum(pos, float(w))) - ug
        mixed.append(_dot(pooled.astype(BF16), pw_ref[gi]))
    mixed = jnp.concatenate(mixed, axis=1) * ps_ref[...]
    y_pool = _dot(mixed.astype(BF16), pup_ref[...])

    q_pre = _dot(h, win_ref[:, off_q:off_f])
    q = q_pre * _sigmoid(q_pre)
    fp = _dot(h, win_ref[:, off_f:off_i])
    v = _dot(h, win_ref[:, off_i:off_g])
    g_pre = _dot(h, win_ref[:, off_g:off_gates])
    g = g_pre * _sigmoid(g_pre)

    en = jnp.exp(-jnp.abs(fp))
    rden = 1.0 / (1.0 + en)
    nonneg = fp >= 0
    spos = jnp.where(nonneg, 1.0, en) * rden
    sneg = jnp.where(nonneg, en, 1.0) * rden
    c = jnp.log(lbf_ref[...] + oml_ref[...] * spos)
    k = oml_ref[...] * sneg + dl_ref[...]

    rows = lax.broadcasted_iota(jnp.int32, (tm, 1), 0)
    cols = lax.broadcasted_iota(jnp.int32, (1, tm), 1)
    v16 = v.astype(BF16)
    scores = [jnp.zeros((tm, tm), F32) for _ in range(HGRN_HEADS)]

    p = c
    tot = c
    half = 1
    shift = 1
    while half < tm:
        upper = (rows & half) != 0
        e = jnp.exp(jnp.where(upper, p, tot - p))
        same_block = (rows >> shift) == (cols >> shift)
        for hi in range(HGRN_HEADS):
            sl = slice(hi * hd, (hi + 1) * hd)
            eh = e[:, sl]
            qt = jnp.where(upper, q[:, sl] * eh, 0.0).astype(BF16)
            kt = jnp.where(upper, 0.0, k[:, sl] * eh).astype(BF16)
            scores[hi] = scores[hi] + jnp.where(same_block, _dot_nt(qt, kt), 0.0)
        other = jnp.where(upper, pltpu.roll(tot, half, 0), pltpu.roll(tot, tm - half, 0))
        p = p + jnp.where(upper, other, 0.0)
        tot = tot + other
        half *= 2
        shift += 1

    eg = jnp.exp(p)
    er = jnp.exp(tot - p)
    o_heads = []
    for hi in range(HGRN_HEADS):
        sl = slice(hi * hd, (hi + 1) * hd)
        qh, kh, vh = q[:, sl], k[:, sl], v[:, sl]
        st = state_ref[hi]
        o = _dot(scores[hi].astype(BF16), v16[:, sl])
        o = o + jnp.sum(qh * kh, axis=-1, keepdims=True) * vh
        o = o + _dot_nt((qh * eg[:, sl]).astype(BF16), st.astype(BF16))
        kd = (kh * er[:, sl]).astype(BF16)
        state_ref[hi] = st * jnp.exp(tot[0:1, sl]) + _dot_tn(v16[:, sl], kd)
        o = o * lax.rsqrt(jnp.mean(o * o, axis=-1, keepdims=True) + EPS)
        o_heads.append(o)
    o_all = jnp.concatenate(o_heads, axis=1) * hnw_ref[...] * g
    y_hgrn = _dot(o_all.astype(BF16), hup_ref[...])

    gz = _dot(h, win_ref[:, off_gates:off_gates + 2 * d]) + gb_ref[...]
    gates = _sigmoid(gz)
    merged = gates[:, 0:d] * y_pool + gates[:, d:2 * d] * y_hgrn
    o_ref[0] = x + _dot(merged.astype(BF16), wout_ref[...])


def _token_mixer(x, nw, win, gb, pw, ps, pup, lbf, oml, dl, hnw, hup, wout):
    b, s, d = x.shape
    tm = MIXER_TM
    pw_cols = pup.shape[0]
    hw = hup.shape[0]
    hd = hw // HGRN_HEADS
    assert s % tm == 0 and tm >= POOL_CARRY
    consts = (nw, win, gb, pw, ps, pup, lbf, oml, dl, hnw, hup, wout)
    return pl.pallas_call(
        _mixer_kernel,
        grid=(b, s // tm),
        in_specs=[pl.BlockSpec((1, tm, d), lambda bi, si: (bi, si, 0))]
        + [_const_spec(a.shape) for a in consts],
        out_specs=pl.BlockSpec((1, tm, d), lambda bi, si: (bi, si, 0)),
        out_shape=jax.ShapeDtypeStruct(x.shape, F32),
        scratch_shapes=[pltpu.VMEM((POOL_CARRY, pw_cols), F32),
                        pltpu.VMEM((HGRN_HEADS, hd, hd), F32)],
        compiler_params=pltpu.CompilerParams(
            dimension_semantics=("arbitrary", "arbitrary"),
            vmem_limit_bytes=VMEM_LIMIT),
        name="token_mixer",
    )(x, *consts)


def _dense_ffn_kernel(x_ref, nw_ref, wg_ref, wu_ref, wd_ref, o_ref, *, n_chunks):
    x = x_ref[...]
    h = _rmsnorm(x, nw_ref[...]).astype(BF16)
    ff = wg_ref.shape[1]
    fc = ff // n_chunks
    out = x
    for ci in range(n_chunks):
        sl = slice(ci * fc, (ci + 1) * fc)
        gt = _dot(h, wg_ref[:, sl])
        up = _dot(h, wu_ref[:, sl])
        a = (gt * _sigmoid(gt) * up).astype(BF16)
        out = out + _dot(a, wd_ref[sl, :])
    o_ref[...] = out


def _dense_ffn(x2, nw, wg, wu, wd):
    t, d = x2.shape
    tm = DENSE_TM
    ff = wg.shape[1]
    n_chunks = 2 if ff % (2 * LANE) == 0 else 1
    assert t % tm == 0
    return pl.pallas_call(
        functools.partial(_dense_ffn_kernel, n_chunks=n_chunks),
        grid=(t // tm,),
        in_specs=[pl.BlockSpec((tm, d), lambda i: (i, 0)),
                  _const_spec(nw.shape), _const_spec(wg.shape),
                  _const_spec(wu.shape), _const_spec(wd.shape)],
        out_specs=pl.BlockSpec((tm, d), lambda i: (i, 0)),
        out_shape=jax.ShapeDtypeStruct(x2.shape, F32),
        compiler_params=pltpu.CompilerParams(
            dimension_semantics=("arbitrary",), vmem_limit_bytes=VMEM_LIMIT),
        name="dense_ffn",
    )(x2, nw, wg, wu, wd)


def _router_kernel(x_ref, nw_ref, r_ref, h_ref, idx_ref, w_ref):
    x = x_ref[...]
    h = _rmsnorm(x, nw_ref[...])
    h_ref[...] = h
    logits = lax.dot_general(h, r_ref[...], (((1,), (0,)), ((), ())),
                             precision=lax.Precision.HIGHEST,
                             preferred_element_type=F32)
    ne = logits.shape[1]
    lane = lax.broadcasted_iota(jnp.int32, logits.shape, 1)
    m1 = jnp.max(logits, axis=-1, keepdims=True)
    i1 = jnp.min(jnp.where(logits == m1, lane, ne), axis=-1, keepdims=True)
    rest = jnp.where(lane == i1, -jnp.inf, logits)
    m2 = jnp.max(rest, axis=-1, keepdims=True)
    i2 = jnp.min(jnp.where(rest == m2, lane, ne), axis=-1, keepdims=True)
    e2 = jnp.exp(m2 - m1)
    den = 1.0 + e2
    slot = lax.broadcasted_iota(jnp.int32, (x.shape[0], TOP_K), 1)
    idx_ref[...] = jnp.where(slot == 0, i1, i2)
    w_ref[...] = jnp.where(slot == 0, 1.0 / den, e2 / den)


def _router(x2, nw, router):
    t, d = x2.shape
    tm = ROUTER_TM
    assert t % tm == 0
    return pl.pallas_call(
        _router_kernel,
        grid=(t // tm,),
        in_specs=[pl.BlockSpec((tm, d), lambda i: (i, 0)),
                  _const_spec(nw.shape), _const_spec(router.shape)],
        out_specs=[pl.BlockSpec((tm, d), lambda i: (i, 0)),
                   pl.BlockSpec((tm, TOP_K), lambda i: (i, 0)),
                   pl.BlockSpec((tm, TOP_K), lambda i: (i, 0))],
        out_shape=[jax.ShapeDtypeStruct((t, d), F32),
                   jax.ShapeDtypeStruct((t, TOP_K), jnp.int32),
                   jax.ShapeDtypeStruct((t, TOP_K), F32)],
        compiler_params=pltpu.CompilerParams(
            dimension_semantics=("arbitrary",), vmem_limit_bytes=VMEM_LIMIT),
        name="moe_router",
    )(x2, nw, router)


def _row_copy(src_hbm, src_row, dst_buf, dst_row, sem):
    return pltpu.make_async_copy(src_hbm.at[pl.ds(src_row, 1), :],
                                 dst_buf.at[pl.ds(dst_row, 1), :], sem)


def _start_row_gather(idx_ref, n, src_hbm, dst_buf, sem):
    def body(r, carry):
        _row_copy(src_hbm, idx_ref[0, 0, r], dst_buf, r, sem).start()
        return carry
    lax.fori_loop(0, n, body, 0)


def _wait_row_gather(n, src_hbm, dst_buf, sem):
    def body(r, carry):
        _row_copy(src_hbm, 0, dst_buf, r, sem).wait()
        return carry
    lax.fori_loop(0, n, body, 0)


def _moe_kernel(blk_e_ref, nblk_ref, idx_ref, idx_next_ref, wrow_ref, h_hbm,
                wg_ref, wu_ref, wd_ref, o_ref, xbuf, xb16, acc, sem):
    del blk_e_ref
    i = pl.program_id(0)
    f = pl.program_id(1)
    nf = pl.num_programs(1)
    bm = xb16.shape[0]
    nblk = nblk_ref[0]
    active = i < nblk
    slot = i % 2

    @pl.when(jnp.logical_and(active, f == 0))
    def _():
        @pl.when(i == 0)
        def _():
            _start_row_gather(idx_ref, bm, h_hbm, xbuf.at[0], sem.at[0])

        @pl.when(slot == 0)
        def _():
            _wait_row_gather(bm, h_hbm, xbuf.at[0], sem.at[0])
            xb16[...] = xbuf[0].astype(BF16)

        @pl.when(slot == 1)
        def _():
            _wait_row_gather(bm, h_hbm, xbuf.at[1], sem.at[1])
            xb16[...] = xbuf[1].astype(BF16)

        @pl.when(i + 1 < nblk)
        def _():
            @pl.when(slot == 0)
            def _():
                _start_row_gather(idx_next_ref, bm, h_hbm, xbuf.at[1], sem.at[1])

            @pl.when(slot == 1)
            def _():
                _start_row_gather(idx_next_ref, bm, h_hbm, xbuf.at[0], sem.at[0])

    @pl.when(active)
    def _():
        xb = xb16[...]
        gt = _dot(xb, wg_ref[0])
        up = _dot(xb, wu_ref[0])
        a = (gt * _sigmoid(gt) * up).astype(BF16)
        part = _dot(a, wd_ref[0])

        @pl.when(f == 0)
        def _():
            acc[...] = part

        @pl.when(f > 0)
        def _():
            acc[...] = acc[...] + part

    @pl.when(f == nf - 1)
    def _():
        @pl.when(active)
        def _():
            o_ref[...] = acc[...] * wrow_ref[...]

        @pl.when(jnp.logical_not(active))
        def _():
            o_ref[...] = jnp.zeros_like(o_ref)


def _moe_experts(h, row_tok, row_w, blk_e, nblk, wg, wu, wd):
    t, d = h.shape
    bm, tf = MOE_BM, MOE_TF
    nb = blk_e.shape[0]
    ff = wg.shape[2]
    assert ff % tf == 0
    nf = ff // tf
    idx3 = row_tok.reshape(nb, 1, bm)
    grid_spec = pltpu.PrefetchScalarGridSpec(
        num_scalar_prefetch=2,
        grid=(nb, nf),
        in_specs=[
            pl.BlockSpec((1, 1, bm), lambda i, f, be, nk: (i, 0, 0), memory_space=pltpu.SMEM),
            pl.BlockSpec((1, 1, bm), lambda i, f, be, nk: (jnp.minimum(i + 1, nb - 1), 0, 0),
                         memory_space=pltpu.SMEM),
            pl.BlockSpec((bm, 1), lambda i, f, be, nk: (i, 0)),
            pl.BlockSpec(memory_space=pl.ANY),
            pl.BlockSpec((1, d, tf), lambda i, f, be, nk: (be[i], 0, f)),
            pl.BlockSpec((1, d, tf), lambda i, f, be, nk: (be[i], 0, f)),
            pl.BlockSpec((1, tf, d), lambda i, f, be, nk: (be[i], f, 0)),
        ],
        out_specs=pl.BlockSpec((bm, d), lambda i, f, be, nk: (i, 0)),
        scratch_shapes=[pltpu.VMEM((2, bm, d), F32),
                        pltpu.VMEM((bm, d), BF16),
                        pltpu.VMEM((bm, d), F32),
                        pltpu.SemaphoreType.DMA((2,))],
    )
    return pl.pallas_call(
        _moe_kernel,
        grid_spec=grid_spec,
        out_shape=jax.ShapeDtypeStruct((nb * bm, d), F32),
        compiler_params=pltpu.CompilerParams(
            dimension_semantics=("arbitrary", "arbitrary"),
            vmem_limit_bytes=VMEM_LIMIT),
        name="moe_experts",
    )(blk_e, nblk, idx3, idx3, row_w.reshape(nb * bm, 1), h, wg, wu, wd)


def _combine_kernel(pos_ref, pos_next_ref, x_ref, nw_ref, y_hbm, o_ref, ybuf, sem, *, apply_norm):
    i = pl.program_id(0)
    n = pl.num_programs(0)
    rows = ybuf.shape[1]
    tm = x_ref.shape[0]
    slot = i % 2

    @pl.when(i == 0)
    def _():
        _start_row_gather(pos_ref, rows, y_hbm, ybuf.at[0], sem.at[0])

    @pl.when(i + 1 < n)
    def _():
        @pl.when(slot == 0)
        def _():
            _start_row_gather(pos_next_ref, rows, y_hbm, ybuf.at[1], sem.at[1])

        @pl.when(slot == 1)
        def _():
            _start_row_gather(pos_next_ref, rows, y_hbm, ybuf.at[0], sem.at[0])

    def finish(s):
        _wait_row_gather(rows, y_hbm, ybuf.at[s], sem.at[s])
        y = ybuf[s]
        z = x_ref[...] + y[0:tm, :] + y[tm:2 * tm, :]
        o_ref[...] = _rmsnorm(z, nw_ref[...]) if apply_norm else z

    @pl.when(slot == 0)
    def _():
        finish(0)

    @pl.when(slot == 1)
    def _():
        finish(1)


def _moe_combine(x2, y_sorted, pos, nw, apply_norm):
    t, d = x2.shape
    tm = COMBINE_TM
    n = t // tm
    pos3 = pos.reshape(n, tm, TOP_K).transpose(0, 2, 1).reshape(n, 1, TOP_K * tm)
    return pl.pallas_call(
        functools.partial(_combine_kernel, apply_norm=apply_norm),
        grid=(n,),
        in_specs=[
            pl.BlockSpec((1, 1, TOP_K * tm), lambda i: (i, 0, 0), memory_space=pltpu.SMEM),
            pl.BlockSpec((1, 1, TOP_K * tm), lambda i: (jnp.minimum(i + 1, n - 1), 0, 0),
                         memory_space=pltpu.SMEM),
            pl.BlockSpec((tm, d), lambda i: (i, 0)),
            _const_spec(nw.shape),
            pl.BlockSpec(memory_space=pl.ANY),
        ],
        out_specs=pl.BlockSpec((tm, d), lambda i: (i, 0)),
        out_shape=jax.ShapeDtypeStruct((t, d), F32),
        scratch_shapes=[pltpu.VMEM((2, TOP_K * tm, d), F32),
                        pltpu.SemaphoreType.DMA((2,))],
        compiler_params=pltpu.CompilerParams(
            dimension_semantics=("arbitrary",), vmem_limit_bytes=VMEM_LIMIT),
        name="moe_combine",
    )(pos3, pos3, x2, nw, y_sorted)


def _route(top_i, top_w, n_experts, bm):
    t = top_i.shape[0]
    na = t * TOP_K
    e = top_i.reshape(na)
    onehot = (e[:, None] == jnp.arange(n_experts, dtype=jnp.int32)[None, :]).astype(jnp.int32)
    csum = jnp.cumsum(onehot, axis=0)
    rank = jnp.sum(csum * onehot, axis=1) - 1
    counts = csum[-1]
    nb_e = (counts + bm - 1) // bm
    blk_end = jnp.cumsum(nb_e)
    blk_start = blk_end - nb_e
    pos = blk_start[e] * bm + rank
    nb = na // bm + n_experts
    blk_e = jnp.minimum(
        jnp.searchsorted(blk_end, jnp.arange(nb, dtype=jnp.int32), side="right"),
        n_experts - 1).astype(jnp.int32)
    tok = jnp.arange(na, dtype=jnp.int32) // TOP_K
    row_tok = jnp.zeros((nb * bm,), jnp.int32).at[pos].set(tok)
    row_w = jnp.zeros((nb * bm,), F32).at[pos].set(top_w.reshape(na))
    nblk = blk_end[-1:].astype(jnp.int32)
    return row_tok, row_w, blk_e, nblk, pos.reshape(t, TOP_K).astype(jnp.int32)


def _final_norm_kernel(x_ref, nw_ref, o_ref):
    o_ref[...] = _rmsnorm(x_ref[...], nw_ref[...])


def _final_norm(x2, nw):
    t, d = x2.shape
    tm = DENSE_TM
    return pl.pallas_call(
        _final_norm_kernel,
        grid=(t // tm,),
        in_specs=[pl.BlockSpec((tm, d), lambda i: (i, 0)), _const_spec(nw.shape)],
        out_specs=pl.BlockSpec((tm, d), lambda i: (i, 0)),
        out_shape=jax.ShapeDtypeStruct((t, d), F32),
        name="final_norm",
    )(x2, nw)


def kernel(x, attn_norm_w, w_in, gate_bias, pool_w, pool_scale, pool_up, hgrn_lower_bounds,
           hgrn_norm_w, hgrn_up, w_out, ffn_norm_w, dense_w_gate, dense_w_up, dense_w_down,
           moe_router, moe_w_gate, moe_w_up, moe_w_down, final_norm_w):
    b, s, d = x.shape
    depth = w_in.shape[0]
    n_experts = moe_router.shape[-1]

    lb_soft = jax.nn.softmax(hgrn_lower_bounds.astype(F32), axis=0)
    lb = jnp.clip(jnp.cumsum(lb_soft, axis=0) - lb_soft[0:1], 0.0, 1.0 - 1e-6)
    lbf = jnp.maximum(lb, LB_FLOOR)
    oml = 1.0 - lb
    dlb = lb - lbf

    def row(a):
        return a.reshape(1, -1).astype(F32)

    final_w = row(final_norm_w)
    normed = None
    for l in range(depth):
        x = _token_mixer(
            x, row(attn_norm_w[l]), w_in[l].astype(BF16), row(gate_bias[l]),
            pool_w[l].astype(BF16), row(pool_scale[l]), pool_up[l].astype(BF16),
            row(lbf[l]), row(oml[l]), row(dlb[l]),
            row(jnp.tile(hgrn_norm_w[l], HGRN_HEADS)), hgrn_up[l].astype(BF16),
            w_out[l].astype(BF16))
        x2 = x.reshape(b * s, d)
        j = l // 2
        last = l == depth - 1
        if l % 2 == 0:
            x2 = _dense_ffn(x2, row(ffn_norm_w[l]), dense_w_gate[j].astype(BF16),
                            dense_w_up[j].astype(BF16), dense_w_down[j].astype(BF16))
            if last:
                normed = _final_norm(x2, final_w)
        else:
            h, top_i, top_w = _router(x2, row(ffn_norm_w[l]), moe_router[j].astype(F32))
            row_tok, row_w, blk_e, nblk, pos = _route(top_i, top_w, n_experts, MOE_BM)
            y_sorted = _moe_experts(h, row_tok, row_w, blk_e, nblk,
                                    moe_w_gate[j].astype(BF16), moe_w_up[j].astype(BF16),
                                    moe_w_down[j].astype(BF16))
            x2 = _moe_combine(x2, y_sorted, pos, final_w, apply_norm=last)
            if last:
                normed = x2
        x = x2.reshape(b, s, d)
    return normed.reshape(b, s, d)
```

```python
import functools

import jax
import jax.numpy as jnp
from jax import lax
from jax.experimental import pallas as pl
from jax.experimental.pallas import tpu as pltpu

F32 = jnp.float32
BF16 = jnp.bfloat16

EPS = 1e-6
LB_FLOOR = 1e-30
POOL_WINDOWS = (2, 4, 8, 16)
N_POOL_GROUPS = 4
HGRN_HEADS = 4
TOP_K = 2
LANE = 128
BF16_SUBLANES = 16

MIXER_TM = 256
POOL_CARRY = 16
DENSE_TM = 512
ROUTER_TM = 512
MOE_BM = 512
MOE_TF = 896
MOE_TC = 512
COMBINE_WIN = 256
VMEM_LIMIT = 56 * 1024 * 1024


def _const_spec(shape):
    nd = len(shape)
    return pl.BlockSpec(shape, lambda *_: (0,) * nd, pipeline_mode=pl.Buffered(1))


def _dot(a, b):
    return jnp.dot(a, b, preferred_element_type=F32)


def _dot_nt(a, b):
    return lax.dot_general(a, b, (((1,), (1,)), ((), ())), preferred_element_type=F32)


def _dot_tn(a, b):
    return lax.dot_general(a, b, (((0,), (0,)), ((), ())), preferred_element_type=F32)


def _sigmoid(x):
    return 1.0 / (1.0 + jnp.exp(-x))


def _rmsnorm(x, w):
    ms = jnp.mean(x * x, axis=-1, keepdims=True)
    return x * lax.rsqrt(ms + EPS) * w


def _mixer_kernel(x_ref, nw_ref, win_ref, gb_ref, pw_ref, ps_ref, pup_ref,
                  lbf_ref, oml_ref, dl_ref, hnw_ref, hup_ref, wout_ref,
                  o_ref, carry_ref, state_ref):
    tm = x_ref.shape[1]
    d = x_ref.shape[2]
    pw_cols = pup_ref.shape[0]
    hw = hup_ref.shape[0]
    gd = pw_cols // N_POOL_GROUPS
    hd = hw // HGRN_HEADS
    off_q = pw_cols
    off_f = off_q + hw
    off_i = off_f + hw
    off_g = off_i + hw
    off_gates = off_g + hw
    si = pl.program_id(1)

    @pl.when(si == 0)
    def _():
        carry_ref[...] = jnp.zeros_like(carry_ref)
        state_ref[...] = jnp.zeros_like(state_ref)

    x = x_ref[0]
    h = _rmsnorm(x, nw_ref[...]).astype(BF16)

    u = _dot(h, win_ref[:, 0:pw_cols])
    ext = jnp.concatenate([carry_ref[...], u], axis=0)
    carry_ref[...] = u[tm - POOL_CARRY:, :]
    pos = (si * tm + 1 + lax.broadcasted_iota(jnp.int32, (tm, 1), 0)).astype(F32)
    acc = ext
    span = 1
    mixed = []
    for gi, w in enumerate(POOL_WINDOWS):
        while span < w:
            acc = acc + pltpu.roll(acc, span, 0)
            span *= 2
        wsum = acc[POOL_CARRY:, 0:gd]
        acc = acc[:, gd:] if gi + 1 < N_POOL_GROUPS else None
        ug = u[:, gi * gd:(gi + 1) * gd]
        pooled = wsum * (1.0 / jnp.minimum(pos, float(w))) - ug
        mixed.append(_dot(pooled.astype(BF16), pw_ref[gi]))
    mixed = jnp.concatenate(mixed, axis=1) * ps_ref[...]
    y_pool = _dot(mixed.astype(BF16), pup_ref[...])

    q_pre = _dot(h, win_ref[:, off_q:off_f])
    q = q_pre * _sigmoid(q_pre)
    fp = _dot(h, win_ref[:, off_f:off_i])
    v = _dot(h, win_ref[:, off_i:off_g])
    g_pre = _dot(h, win_ref[:, off_g:off_gates])
    g = g_pre * _sigmoid(g_pre)

    en = jnp.exp(-jnp.abs(fp))
    rden = 1.0 / (1.0 + en)
    nonneg = fp >= 0
    spos = jnp.where(nonneg, 1.0, en) * rden
    sneg = jnp.where(nonneg, en, 1.0) * rden
    c = jnp.log(lbf_ref[...] + oml_ref[...] * spos)
    k = oml_ref[...] * sneg + dl_ref[...]

    rows = lax.broadcasted_iota(jnp.int32, (tm, 1), 0)
    cols = lax.broadcasted_iota(jnp.int32, (1, tm), 1)
    v16 = v.astype(BF16)
    scores = [jnp.zeros((tm, tm), F32) for _ in range(HGRN_HEADS)]

    p = c
    tot = c
    half = 1
    shift = 1
    while half < tm:
        upper = (rows & half) != 0
        e = jnp.exp(jnp.where(upper, p, tot - p))
        same_block = (rows >> shift) == (cols >> shift)
        for hi in range(HGRN_HEADS):
            sl = slice(hi * hd, (hi + 1) * hd)
            eh = e[:, sl]
            qt = jnp.where(upper, q[:, sl] * eh, 0.0).astype(BF16)
            kt = jnp.where(upper, 0.0, k[:, sl] * eh).astype(BF16)
            scores[hi] = scores[hi] + jnp.where(same_block, _dot_nt(qt, kt), 0.0)
        other = jnp.where(upper, pltpu.roll(tot, half, 0), pltpu.roll(tot, tm - half, 0))
        p = p + jnp.where(upper, other, 0.0)
        tot = tot + other
        half *= 2
        shift += 1

    eg = jnp.exp(p)
    er = jnp.exp(tot - p)
    o_heads = []
    for hi in range(HGRN_HEADS):
        sl = slice(hi * hd, (hi + 1) * hd)
        qh, kh, vh = q[:, sl], k[:, sl], v[:, sl]
        st = state_ref[hi]
        o = _dot(scores[hi].astype(BF16), v16[:, sl])
        o = o + jnp.sum(qh * kh, axis=-1, keepdims=True) * vh
        o = o + _dot_nt((qh * eg[:, sl]).astype(BF16), st.astype(BF16))
        kd = (kh * er[:, sl]).astype(BF16)
        state_ref[hi] = st * jnp.exp(tot[0:1, sl]) + _dot_tn(v16[:, sl], kd)
        o = o * lax.rsqrt(jnp.mean(o * o, axis=-1, keepdims=True) + EPS)
        o_heads.append(o)
    o_all = jnp.concatenate(o_heads, axis=1) * hnw_ref[...] * g
    y_hgrn = _dot(o_all.astype(BF16), hup_ref[...])

    gz = _dot(h, win_ref[:, off_gates:off_gates + 2 * d]) + gb_ref[...]
    gates = _sigmoid(gz)
    merged = gates[:, 0:d] * y_pool + gates[:, d:2 * d] * y_hgrn
    o_ref[0] = x + _dot(merged.astype(BF16), wout_ref[...])


def _token_mixer(x, nw, win, gb, pw, ps, pup, lbf, oml, dl, hnw, hup, wout):
    b, s, d = x.shape
    tm = MIXER_TM
    pw_cols = pup.shape[0]
    hw = hup.shape[0]
    hd = hw // HGRN_HEADS
    assert s % tm == 0 and tm >= POOL_CARRY
    consts = (nw, win, gb, pw, ps, pup, lbf, oml, dl, hnw, hup, wout)
    return pl.pallas_call(
        _mixer_kernel,
        grid=(b, s // tm),
        in_specs=[pl.BlockSpec((1, tm, d), lambda bi, si: (bi, si, 0))]
        + [_const_spec(a.shape) for a in consts],
        out_specs=pl.BlockSpec((1, tm, d), lambda bi, si: (bi, si, 0)),
        out_shape=jax.ShapeDtypeStruct(x.shape, F32),
        scratch_shapes=[pltpu.VMEM((POOL_CARRY, pw_cols), F32),
                        pltpu.VMEM((HGRN_HEADS, hd, hd), F32)],
        compiler_params=pltpu.CompilerParams(
            dimension_semantics=("arbitrary", "arbitrary"),
            vmem_limit_bytes=VMEM_LIMIT),
        name="token_mixer",
    )(x, *consts)


def _dense_ffn_kernel(x_ref, nw_ref, wg_ref, wu_ref, wd_ref, o_ref, *, n_chunks):
    x = x_ref[...]
    h = _rmsnorm(x, nw_ref[...]).astype(BF16)
    ff = wg_ref.shape[1]
    fc = ff // n_chunks
    out = x
    for ci in range(n_chunks):
        sl = slice(ci * fc, (ci + 1) * fc)
        gt = _dot(h, wg_ref[:, sl])
        up = _dot(h, wu_ref[:, sl])
        a = (gt * _sigmoid(gt) * up).astype(BF16)
        out = out + _dot(a, wd_ref[sl, :])
    o_ref[...] = out


def _dense_ffn(x2, nw, wg, wu, wd):
    t, d = x2.shape
    tm = DENSE_TM
    ff = wg.shape[1]
    n_chunks = 2 if ff % (2 * LANE) == 0 else 1
    assert t % tm == 0
    return pl.pallas_call(
        functools.partial(_dense_ffn_kernel, n_chunks=n_chunks),
        grid=(t // tm,),
        in_specs=[pl.BlockSpec((tm, d), lambda i: (i, 0)),
                  _const_spec(nw.shape), _const_spec(wg.shape),
                  _const_spec(wu.shape), _const_spec(wd.shape)],
        out_specs=pl.BlockSpec((tm, d), lambda i: (i, 0)),
        out_shape=jax.ShapeDtypeStruct(x2.shape, F32),
        compiler_params=pltpu.CompilerParams(
            dimension_semantics=("arbitrary",), vmem_limit_bytes=VMEM_LIMIT),
        name="dense_ffn",
    )(x2, nw, wg, wu, wd)


def _router_kernel(x_ref, nw_ref, r_ref, h_ref, idx_ref, w_ref):
    x = x_ref[...]
    h = _rmsnorm(x, nw_ref[...])
    h_ref[...] = h.astype(BF16)
    logits = lax.dot_general(h, r_ref[...], (((1,), (0,)), ((), ())),
                             precision=lax.Precision.HIGHEST,
                             preferred_element_type=F32)
    ne = logits.shape[1]
    lane = lax.broadcasted_iota(jnp.int32, logits.shape, 1)
    m1 = jnp.max(logits, axis=-1, keepdims=True)
    i1 = jnp.min(jnp.where(logits == m1, lane, ne), axis=-1, keepdims=True)
    rest = jnp.where(lane == i1, -jnp.inf, logits)
    m2 = jnp.max(rest, axis=-1, keepdims=True)
    i2 = jnp.min(jnp.where(rest == m2, lane, ne), axis=-1, keepdims=True)
    e2 = jnp.exp(m2 - m1)
    den = 1.0 + e2
    slot = lax.broadcasted_iota(jnp.int32, (x.shape[0], TOP_K), 1)
    idx_ref[...] = jnp.where(slot == 0, i1, i2)
    w_ref[...] = jnp.where(slot == 0, 1.0 / den, e2 / den)


def _router(x2, nw, router):
    t, d = x2.shape
    tm = ROUTER_TM
    assert t % tm == 0
    return pl.pallas_call(
        _router_kernel,
        grid=(t // tm,),
        in_specs=[pl.BlockSpec((tm, d), lambda i: (i, 0)),
                  _const_spec(nw.shape), _const_spec(router.shape)],
        out_specs=[pl.BlockSpec((tm, d), lambda i: (i, 0)),
                   pl.BlockSpec((tm, TOP_K), lambda i: (i, 0)),
                   pl.BlockSpec((tm, TOP_K), lambda i: (i, 0))],
        out_shape=[jax.ShapeDtypeStruct((t, d), BF16),
                   jax.ShapeDtypeStruct((t, TOP_K), jnp.int32),
                   jax.ShapeDtypeStruct((t, TOP_K), F32)],
        compiler_params=pltpu.CompilerParams(
            dimension_semantics=("arbitrary",), vmem_limit_bytes=VMEM_LIMIT),
        name="moe_router",
    )(x2, nw, router)


def _route(top_i, n_experts, bm, tc):
    t = top_i.shape[0]
    ex = jnp.arange(n_experts, dtype=jnp.int32)
    sel = top_i[:, :, None] == ex[None, None, :]
    chosen = jnp.any(sel, axis=1).astype(jnp.int32)
    csum = jnp.cumsum(chosen, axis=0)
    counts = csum[-1]
    nb_e = (counts + bm - 1) // bm
    blk_end = jnp.cumsum(nb_e)
    blk_start = blk_end - nb_e
    row0 = blk_start * bm
    seli = sel.astype(jnp.int32)
    pos = jnp.sum(seli * (csum - 1 + row0[None, :])[:, None, :], axis=2)

    nb = (t * TOP_K) // bm + n_experts
    bi = jnp.arange(nb, dtype=jnp.int32)
    blk_e = jnp.minimum(jnp.sum((bi[:, None] >= blk_end[None, :]).astype(jnp.int32), axis=1),
                        n_experts - 1)
    nblk = blk_end[-1]
    active = bi < nblk

    chunk_end = csum[tc - 1::tc]
    chunk_prev = jnp.concatenate([jnp.zeros((1, n_experts), jnp.int32), chunk_end[:-1]], axis=0)
    o_lo = (bi - blk_start[blk_e]) * bm
    o_hi = jnp.minimum(o_lo + bm, counts[blk_e])
    ce_blk = chunk_end.T[blk_e]
    c_lo = jnp.sum((ce_blk <= o_lo[:, None]).astype(jnp.int32), axis=1)
    c_hi = jnp.sum((ce_blk < o_hi[:, None]).astype(jnp.int32), axis=1)
    n_ch = jnp.where(active & (o_hi > o_lo), c_hi - c_lo + 1, 0)
    c_lo = jnp.where(n_ch > 0, c_lo, 0)

    pair_row = row0[None, :] + chunk_prev
    pair_cnt = chunk_end - chunk_prev
    i32 = lambda a: a.astype(jnp.int32)
    return (i32(pos), i32(blk_e), i32(nblk).reshape(1), i32(c_lo), i32(n_ch),
            i32(pair_row), i32(pair_cnt))


def _chunk_copies(h_hbm, pos_hbm, hbuf, pbuf, sem, chunk, slot):
    return (pltpu.make_async_copy(h_hbm.at[chunk], hbuf.at[slot], sem.at[0, slot]),
            pltpu.make_async_copy(pos_hbm.at[chunk], pbuf.at[slot], sem.at[1, slot]))


def _gather_kernel(c_lo_ref, n_ch_ref, h_hbm, pos_hbm, o_ref, hbuf, pbuf, acc, sem):
    i = pl.program_id(0)
    bm = o_ref.shape[0]
    tc = hbuf.shape[1]
    c0 = c_lo_ref[i]
    n = n_ch_ref[i]
    acc[...] = jnp.zeros_like(acc)

    @pl.when(n > 0)
    def _():
        for cp in _chunk_copies(h_hbm, pos_hbm, hbuf, pbuf, sem, c0, 0):
            cp.start()

    base = i * bm
    rows = lax.broadcasted_iota(jnp.int32, (bm, 1), 0) + base

    def body(j, carry):
        slot = j % 2

        @pl.when(j + 1 < n)
        def _():
            for cp in _chunk_copies(h_hbm, pos_hbm, hbuf, pbuf, sem, c0 + j + 1, 1 - slot):
                cp.start()

        for cp in _chunk_copies(h_hbm, pos_hbm, hbuf, pbuf, sem, c0 + j, slot):
            cp.wait()
        pc = pbuf[slot]
        hit = rows == pc[0:1, :]
        for kk in range(1, TOP_K):
            hit = jnp.logical_or(hit, rows == pc[kk:kk + 1, :])
        onehot = jnp.where(hit, 1.0, 0.0).astype(BF16)
        acc[...] += _dot(onehot, hbuf[slot])
        return carry

    lax.fori_loop(0, n, body, 0)
    o_ref[...] = acc[...].astype(o_ref.dtype)


def _moe_gather(h, pos, c_lo, n_ch, nb, bm, tc):
    t, d = h.shape
    nc = t // tc
    h3 = h.reshape(nc, tc, d)
    pos3 = pos.reshape(nc, tc, TOP_K).transpose(0, 2, 1)
    grid_spec = pltpu.PrefetchScalarGridSpec(
        num_scalar_prefetch=2,
        grid=(nb,),
        in_specs=[pl.BlockSpec(memory_space=pl.ANY), pl.BlockSpec(memory_space=pl.ANY)],
        out_specs=pl.BlockSpec((bm, d), lambda i, a, b: (i, 0)),
        scratch_shapes=[pltpu.VMEM((2, tc, d), BF16),
                        pltpu.VMEM((2, TOP_K, tc), jnp.int32),
                        pltpu.VMEM((bm, d), F32),
                        pltpu.SemaphoreType.DMA((2, 2))],
    )
    return pl.pallas_call(
        _gather_kernel,
        grid_spec=grid_spec,
        out_shape=jax.ShapeDtypeStruct((nb * bm, d), BF16),
        compiler_params=pltpu.CompilerParams(
            dimension_semantics=("arbitrary",), vmem_limit_bytes=VMEM_LIMIT),
        name="moe_gather",
    )(c_lo, n_ch, h3, pos3)


def _experts_kernel(blk_e_ref, nblk_ref, x_ref, wg_ref, wu_ref, wd_ref, o_ref, acc):
    del blk_e_ref
    i = pl.program_id(0)
    f = pl.program_id(1)
    nf = pl.num_programs(1)
    active = i < nblk_ref[0]

    @pl.when(active)
    def _():
        xb = x_ref[...]
        gt = _dot(xb, wg_ref[0])
        up = _dot(xb, wu_ref[0])
        a = (gt * _sigmoid(gt) * up).astype(BF16)
        part = _dot(a, wd_ref[0])

        @pl.when(f == 0)
        def _():
            acc[...] = part

        @pl.when(f > 0)
        def _():
            acc[...] = acc[...] + part

    @pl.when(f == nf - 1)
    def _():
        @pl.when(active)
        def _():
            o_ref[...] = acc[...].astype(o_ref.dtype)

        @pl.when(jnp.logical_not(active))
        def _():
            o_ref[...] = jnp.zeros_like(o_ref)


def _moe_experts(x_sorted, blk_e, nblk, wg, wu, wd, bm):
    rows, d = x_sorted.shape
    nb = rows // bm
    tf = MOE_TF
    ff = wg.shape[2]
    assert ff % tf == 0
    nf = ff // tf
    grid_spec = pltpu.PrefetchScalarGridSpec(
        num_scalar_prefetch=2,
        grid=(nb, nf),
        in_specs=[
            pl.BlockSpec((bm, d), lambda i, f, be, nk: (i, 0)),
            pl.BlockSpec((1, d, tf), lambda i, f, be, nk: (be[i], 0, f)),
            pl.BlockSpec((1, d, tf), lambda i, f, be, nk: (be[i], 0, f)),
            pl.BlockSpec((1, tf, d), lambda i, f, be, nk: (be[i], f, 0)),
        ],
        out_specs=pl.BlockSpec((bm, d), lambda i, f, be, nk: (i, 0)),
        scratch_shapes=[pltpu.VMEM((bm, d), F32)],
    )
    return pl.pallas_call(
        _experts_kernel,
        grid_spec=grid_spec,
        out_shape=jax.ShapeDtypeStruct((rows, d), BF16),
        compiler_params=pltpu.CompilerParams(
            dimension_semantics=("arbitrary", "arbitrary"),
            vmem_limit_bytes=VMEM_LIMIT),
        name="moe_experts",
    )(blk_e, nblk, x_sorted, wg, wu, wd)


def _window_start(row, n_rows):
    a0 = (row // BF16_SUBLANES) * BF16_SUBLANES
    return pl.multiple_of(jnp.minimum(a0, n_rows - COMBINE_WIN), BF16_SUBLANES)


def _window_copy(y_hbm, ybuf, sem, start, slot, e):
    return pltpu.make_async_copy(y_hbm.at[pl.ds(start, COMBINE_WIN), :],
                                 ybuf.at[slot, e], sem.at[slot, e])


def _combine_kernel(row_ref, cnt_ref, x_ref, idx_ref, w_ref, pos_ref, nw_ref, y_hbm,
                    o_ref, ybuf, xtra, sem, xsem, *, apply_norm):
    j = pl.program_id(0)
    nt = pl.num_programs(0)
    ne = ybuf.shape[1]
    n_rows = y_hbm.shape[0]
    slot = j % 2

    def start_tile(tile, s):
        for e in range(ne):
            _window_copy(y_hbm, ybuf, sem, _window_start(row_ref[tile * ne + e], n_rows), s, e).start()

    @pl.when(j == 0)
    def _():
        start_tile(0, 0)

    @pl.when(j + 1 < nt)
    def _():
        @pl.when(slot == 0)
        def _():
            start_tile(j + 1, 1)

        @pl.when(slot == 1)
        def _():
            start_tile(j + 1, 0)

    idx = idx_ref[...]
    wts = w_ref[...]
    pos = pos_ref[...]
    lanes = lax.broadcasted_iota(jnp.int32, (1, COMBINE_WIN), 1)

    def window_term(e, start, win):
        hit = None
        coef = None
        for kk in range(TOP_K):
            mine = idx[:, kk:kk + 1] == e
            hk = jnp.logical_and(mine, (pos[:, kk:kk + 1] - start) == lanes)
            ck = jnp.where(mine, wts[:, kk:kk + 1], 0.0)
            hit = hk if hit is None else jnp.logical_or(hit, hk)
            coef = ck if coef is None else coef + ck
        onehot = jnp.where(hit, 1.0, 0.0).astype(BF16)
        return coef * _dot(onehot, win)

    def finish(s):
        z = x_ref[...]
        for e in range(ne):
            row = row_ref[j * ne + e]
            cnt = cnt_ref[j * ne + e]
            start = _window_start(row, n_rows)
            _window_copy(y_hbm, ybuf, sem, start, s, e).wait()
            z = z + window_term(e, start, ybuf[s, e])
            n_more = (row - start + cnt + COMBINE_WIN - 1) // COMBINE_WIN - 1

            def more(m, zc, e=e, start=start):
                st = pl.multiple_of(jnp.minimum(start + (m + 1) * COMBINE_WIN, n_rows - COMBINE_WIN),
                                    BF16_SUBLANES)
                cp = pltpu.make_async_copy(y_hbm.at[pl.ds(st, COMBINE_WIN), :], xtra, xsem.at[0])
                cp.start()
                cp.wait()
                fresh = st - (start + (m + 1) * COMBINE_WIN)
                term = window_term_masked(e, st, xtra[...], -fresh)
                return zc + term

            z = lax.fori_loop(0, n_more, more, z)
        o_ref[...] = _rmsnorm(z, nw_ref[...]) if apply_norm else z

    def window_term_masked(e, start, win, skip):
        hit = None
        coef = None
        for kk in range(TOP_K):
            mine = idx[:, kk:kk + 1] == e
            off = pos[:, kk:kk + 1] - start
            hk = jnp.logical_and(jnp.logical_and(mine, off == lanes), off >= skip)
            ck = jnp.where(mine, wts[:, kk:kk + 1], 0.0)
            hit = hk if hit is None else jnp.logical_or(hit, hk)
            coef = ck if coef is None else coef + ck
        onehot = jnp.where(hit, 1.0, 0.0).astype(BF16)
        return coef * _dot(onehot, win)

    @pl.when(slot == 0)
    def _():
        finish(0)

    @pl.when(slot == 1)
    def _():
        finish(1)


def _moe_combine(x2, y_sorted, top_i, top_w, pos, pair_row, pair_cnt, nw, apply_norm):
    t, d = x2.shape
    tm = MOE_TC
    nt = t // tm
    ne = pair_row.shape[1]
    assert y_sorted.shape[0] >= COMBINE_WIN
    grid_spec = pltpu.PrefetchScalarGridSpec(
        num_scalar_prefetch=2,
        grid=(nt,),
        in_specs=[
            pl.BlockSpec((tm, d), lambda i, a, b: (i, 0)),
            pl.BlockSpec((tm, TOP_K), lambda i, a, b: (i, 0)),
            pl.BlockSpec((tm, TOP_K), lambda i, a, b: (i, 0)),
            pl.BlockSpec((tm, TOP_K), lambda i, a, b: (i, 0)),
            pl.BlockSpec(nw.shape, lambda i, a, b: (0, 0)),
            pl.BlockSpec(memory_space=pl.ANY),
        ],
        out_specs=pl.BlockSpec((tm, d), lambda i, a, b: (i, 0)),
        scratch_shapes=[pltpu.VMEM((2, ne, COMBINE_WIN, d), BF16),
                        pltpu.VMEM((COMBINE_WIN, d), BF16),
                        pltpu.SemaphoreType.DMA((2, ne)),
                        pltpu.SemaphoreType.DMA((1,))],
    )
    return pl.pallas_call(
        functools.partial(_combine_kernel, apply_norm=apply_norm),
        grid_spec=grid_spec,
        out_shape=jax.ShapeDtypeStruct((t, d), F32),
        compiler_params=pltpu.CompilerParams(
            dimension_semantics=("arbitrary",), vmem_limit_bytes=VMEM_LIMIT),
        name="moe_combine",
    )(pair_row.reshape(-1), pair_cnt.reshape(-1), x2, top_i, top_w, pos, nw, y_sorted)


def _final_norm_kernel(x_ref, nw_ref, o_ref):
    o_ref[...] = _rmsnorm(x_ref[...], nw_ref[...])


def _final_norm(x2, nw):
    t, d = x2.shape
    tm = DENSE_TM
    return pl.pallas_call(
        _final_norm_kernel,
        grid=(t // tm,),
        in_specs=[pl.BlockSpec((tm, d), lambda i: (i, 0)), _const_spec(nw.shape)],
        out_specs=pl.BlockSpec((tm, d), lambda i: (i, 0)),
        out_shape=jax.ShapeDtypeStruct((t, d), F32),
        name="final_norm",
    )(x2, nw)


def kernel(x, attn_norm_w, w_in, gate_bias, pool_w, pool_scale, pool_up, hgrn_lower_bounds,
           hgrn_norm_w, hgrn_up, w_out, ffn_norm_w, dense_w_gate, dense_w_up, dense_w_down,
           moe_router, moe_w_gate, moe_w_up, moe_w_down, final_norm_w):
    b, s, d = x.shape
    depth = w_in.shape[0]
    n_experts = moe_router.shape[-1]

    lb_soft = jax.nn.softmax(hgrn_lower_bounds.astype(F32), axis=0)
    lb = jnp.clip(jnp.cumsum(lb_soft, axis=0) - lb_soft[0:1], 0.0, 1.0 - 1e-6)
    lbf = jnp.maximum(lb, LB_FLOOR)
    oml = 1.0 - lb
    dlb = lb - lbf

    def row(a):
        return a.reshape(1, -1).astype(F32)

    final_w = row(final_norm_w)
    normed = None
    for l in range(depth):
        x = _token_mixer(
            x, row(attn_norm_w[l]), w_in[l].astype(BF16), row(gate_bias[l]),
            pool_w[l].astype(BF16), row(pool_scale[l]), pool_up[l].astype(BF16),
            row(lbf[l]), row(oml[l]), row(dlb[l]),
            row(jnp.tile(hgrn_norm_w[l], HGRN_HEADS)), hgrn_up[l].astype(BF16),
            w_out[l].astype(BF16))
        x2 = x.reshape(b * s, d)
        j = l // 2
        last = l == depth - 1
        if l % 2 == 0:
            x2 = _dense_ffn(x2, row(ffn_norm_w[l]), dense_w_gate[j].astype(BF16),
                            dense_w_up[j].astype(BF16), dense_w_down[j].astype(BF16))
            if last:
                normed = _final_norm(x2, final_w)
        else:
            h, top_i, top_w = _router(x2, row(ffn_norm_w[l]), moe_router[j].astype(F32))
            pos, blk_e, nblk, c_lo, n_ch, pair_row, pair_cnt = _route(
                top_i, n_experts, MOE_BM, MOE_TC)
            x_sorted = _moe_gather(h, pos, c_lo, n_ch, blk_e.shape[0], MOE_BM, MOE_TC)
            y_sorted = _moe_experts(x_sorted, blk_e, nblk, moe_w_gate[j].astype(BF16),
                                    moe_w_up[j].astype(BF16), moe_w_down[j].astype(BF16), MOE_BM)
            x2 = _moe_combine(x2, y_sorted, top_i, top_w, pos, pair_row, pair_cnt, final_w,
                              apply_norm=last)
            if last:
                normed = x2
        x = x2.reshape(b, s, d)
    return normed.reshape(b, s, d)
```

```python
import functools

import jax
import jax.numpy as jnp
from jax import lax
from jax.experimental import pallas as pl
from jax.experimental.pallas import tpu as pltpu

F32 = jnp.float32
BF16 = jnp.bfloat16

EPS = 1e-6
LB_FLOOR = 1e-30
POOL_WINDOWS = (2, 4, 8, 16)
N_POOL_GROUPS = 4
HGRN_HEADS = 4
TOP_K = 2
LANE = 128
BF16_SUBLANES = 16

MIXER_TM = 256
POOL_CARRY = 16
DENSE_TM = 512
ROUTER_TM = 512
MOE_BM = 512
MOE_TF = 1792
MOE_TC = 512
COMBINE_WIN = 256
VMEM_LIMIT = 56 * 1024 * 1024


def _const_spec(shape):
    nd = len(shape)
    return pl.BlockSpec(shape, lambda *_: (0,) * nd, pipeline_mode=pl.Buffered(1))


def _dot(a, b):
    return jnp.dot(a, b, preferred_element_type=F32)


def _dot_nt(a, b):
    return lax.dot_general(a, b, (((1,), (1,)), ((), ())), preferred_element_type=F32)


def _dot_tn(a, b):
    return lax.dot_general(a, b, (((0,), (0,)), ((), ())), preferred_element_type=F32)


def _sigmoid(x):
    return 1.0 / (1.0 + jnp.exp(-x))


def _rmsnorm(x, w):
    ms = jnp.mean(x * x, axis=-1, keepdims=True)
    return x * lax.rsqrt(ms + EPS) * w


def _mixer_kernel(x_ref, nw_ref, win_ref, gb_ref, pw_ref, ps_ref, pup_ref,
                  lbf_ref, oml_ref, dl_ref, hnw_ref, hup_ref, wout_ref,
                  o_ref, carry_ref, state_ref):
    tm = x_ref.shape[1]
    d = x_ref.shape[2]
    pw_cols = pup_ref.shape[0]
    hw = hup_ref.shape[0]
    gd = pw_cols // N_POOL_GROUPS
    hd = hw // HGRN_HEADS
    off_q = pw_cols
    off_f = off_q + hw
    off_i = off_f + hw
    off_g = off_i + hw
    off_gates = off_g + hw
    si = pl.program_id(1)

    @pl.when(si == 0)
    def _():
        carry_ref[...] = jnp.zeros_like(carry_ref)
        state_ref[...] = jnp.zeros_like(state_ref)

    x = x_ref[0]
    h = _rmsnorm(x, nw_ref[...]).astype(BF16)

    u = _dot(h, win_ref[:, 0:pw_cols])
    ext = jnp.concatenate([carry_ref[...], u], axis=0)
    carry_ref[...] = u[tm - POOL_CARRY:, :]
    pos = (si * tm + 1 + lax.broadcasted_iota(jnp.int32, (tm, 1), 0)).astype(F32)
    acc = ext
    span = 1
    mixed = []
    for gi, w in enumerate(POOL_WINDOWS):
        while span < w:
            acc = acc + pltpu.roll(acc, span, 0)
            span *= 2
        wsum = acc[POOL_CARRY:, 0:gd]
        acc = acc[:, gd:] if gi + 1 < N_POOL_GROUPS else None
        ug = u[:, gi * gd:(gi + 1) * gd]
        pooled = wsum * (1.0 / jnp.minimum(pos, float(w))) - ug
        mixed.append(_dot(pooled.astype(BF16), pw_ref[gi]))
    mixed = jnp.concatenate(mixed, axis=1) * ps_ref[...]
    y_pool = _dot(mixed.astype(BF16), pup_ref[...])

    q_pre = _dot(h, win_ref[:, off_q:off_f])
    q = q_pre * _sigmoid(q_pre)
    fp = _dot(h, win_ref[:, off_f:off_i])
    v = _dot(h, win_ref[:, off_i:off_g])
    g_pre = _dot(h, win_ref[:, off_g:off_gates])
    g = g_pre * _sigmoid(g_pre)

    en = jnp.exp(-jnp.abs(fp))
    rden = 1.0 / (1.0 + en)
    nonneg = fp >= 0
    spos = jnp.where(nonneg, 1.0, en) * rden
    sneg = jnp.where(nonneg, en, 1.0) * rden
    c = jnp.log(lbf_ref[...] + oml_ref[...] * spos)
    k = oml_ref[...] * sneg + dl_ref[...]

    rows = lax.broadcasted_iota(jnp.int32, (tm, 1), 0)
    cols = lax.broadcasted_iota(jnp.int32, (1, tm), 1)
    v16 = v.astype(BF16)
    scores = [jnp.zeros((tm, tm), F32) for _ in range(HGRN_HEADS)]

    p = c
    tot = c
    half = 1
    shift = 1
    while half < tm:
        upper = (rows & half) != 0
        e = jnp.exp(jnp.where(upper, p, tot - p))
        same_block = (rows >> shift) == (cols >> shift)
        for hi in range(HGRN_HEADS):
            sl = slice(hi * hd, (hi + 1) * hd)
            eh = e[:, sl]
            qt = jnp.where(upper, q[:, sl] * eh, 0.0).astype(BF16)
            kt = jnp.where(upper, 0.0, k[:, sl] * eh).astype(BF16)
            scores[hi] = scores[hi] + jnp.where(same_block, _dot_nt(qt, kt), 0.0)
        other = jnp.where(upper, pltpu.roll(tot, half, 0), pltpu.roll(tot, tm - half, 0))
        p = p + jnp.where(upper, other, 0.0)
        tot = tot + other
        half *= 2
        shift += 1

    eg = jnp.exp(p)
    er = jnp.exp(tot - p)
    o_heads = []
    for hi in range(HGRN_HEADS):
        sl = slice(hi * hd, (hi + 1) * hd)
        qh, kh, vh = q[:, sl], k[:, sl], v[:, sl]
        st = state_ref[hi]
        o = _dot(scores[hi].astype(BF16), v16[:, sl])
        o = o + jnp.sum(qh * kh, axis=-1, keepdims=True) * vh
        o = o + _dot_nt((qh * eg[:, sl]).astype(BF16), st.astype(BF16))
        kd = (kh * er[:, sl]).astype(BF16)
        state_ref[hi] = st * jnp.exp(tot[0:1, sl]) + _dot_tn(v16[:, sl], kd)
        o = o * lax.rsqrt(jnp.mean(o * o, axis=-1, keepdims=True) + EPS)
        o_heads.append(o)
    o_all = jnp.concatenate(o_heads, axis=1) * hnw_ref[...] * g
    y_hgrn = _dot(o_all.astype(BF16), hup_ref[...])

    gz = _dot(h, win_ref[:, off_gates:off_gates + 2 * d]) + gb_ref[...]
    gates = _sigmoid(gz)
    merged = gates[:, 0:d] * y_pool + gates[:, d:2 * d] * y_hgrn
    o_ref[0] = x + _dot(merged.astype(BF16), wout_ref[...])


def _token_mixer(x, nw, win, gb, pw, ps, pup, lbf, oml, dl, hnw, hup, wout):
    b, s, d = x.shape
    tm = MIXER_TM
    pw_cols = pup.shape[0]
    hw = hup.shape[0]
    hd = hw // HGRN_HEADS
    assert s % tm == 0 and tm >= POOL_CARRY
    consts = (nw, win, gb, pw, ps, pup, lbf, oml, dl, hnw, hup, wout)
    return pl.pallas_call(
        _mixer_kernel,
        grid=(b, s // tm),
        in_specs=[pl.BlockSpec((1, tm, d), lambda bi, si: (bi, si, 0))]
        + [_const_spec(a.shape) for a in consts],
        out_specs=pl.BlockSpec((1, tm, d), lambda bi, si: (bi, si, 0)),
        out_shape=jax.ShapeDtypeStruct(x.shape, F32),
        scratch_shapes=[pltpu.VMEM((POOL_CARRY, pw_cols), F32),
                        pltpu.VMEM((HGRN_HEADS, hd, hd), F32)],
        compiler_params=pltpu.CompilerParams(
            dimension_semantics=("arbitrary", "arbitrary"),
            vmem_limit_bytes=VMEM_LIMIT),
        name="token_mixer",
    )(x, *consts)


def _dense_ffn_kernel(x_ref, nw_ref, wg_ref, wu_ref, wd_ref, o_ref, *, n_chunks):
    x = x_ref[...]
    h = _rmsnorm(x, nw_ref[...]).astype(BF16)
    ff = wg_ref.shape[1]
    fc = ff // n_chunks
    out = x
    for ci in range(n_chunks):
        sl = slice(ci * fc, (ci + 1) * fc)
        gt = _dot(h, wg_ref[:, sl])
        up = _dot(h, wu_ref[:, sl])
        a = (gt * _sigmoid(gt) * up).astype(BF16)
        out = out + _dot(a, wd_ref[sl, :])
    o_ref[...] = out


def _dense_ffn(x2, nw, wg, wu, wd):
    t, d = x2.shape
    tm = DENSE_TM
    ff = wg.shape[1]
    n_chunks = 2 if ff % (2 * LANE) == 0 else 1
    assert t % tm == 0
    return pl.pallas_call(
        functools.partial(_dense_ffn_kernel, n_chunks=n_chunks),
        grid=(t // tm,),
        in_specs=[pl.BlockSpec((tm, d), lambda i: (i, 0)),
                  _const_spec(nw.shape), _const_spec(wg.shape),
                  _const_spec(wu.shape), _const_spec(wd.shape)],
        out_specs=pl.BlockSpec((tm, d), lambda i: (i, 0)),
        out_shape=jax.ShapeDtypeStruct(x2.shape, F32),
        compiler_params=pltpu.CompilerParams(
            dimension_semantics=("arbitrary",), vmem_limit_bytes=VMEM_LIMIT),
        name="dense_ffn",
    )(x2, nw, wg, wu, wd)


def _router_kernel(x_ref, nw_ref, r_ref, h_ref, idx_ref, w_ref):
    x = x_ref[...]
    h = _rmsnorm(x, nw_ref[...])
    h_ref[...] = h.astype(BF16)
    logits = lax.dot_general(h, r_ref[...], (((1,), (0,)), ((), ())),
                             precision=lax.Precision.HIGHEST,
                             preferred_element_type=F32)
    ne = logits.shape[1]
    lane = lax.broadcasted_iota(jnp.int32, logits.shape, 1)
    m1 = jnp.max(logits, axis=-1, keepdims=True)
    i1 = jnp.min(jnp.where(logits == m1, lane, ne), axis=-1, keepdims=True)
    rest = jnp.where(lane == i1, -jnp.inf, logits)
    m2 = jnp.max(rest, axis=-1, keepdims=True)
    i2 = jnp.min(jnp.where(rest == m2, lane, ne), axis=-1, keepdims=True)
    e2 = jnp.exp(m2 - m1)
    den = 1.0 + e2
    slot = lax.broadcasted_iota(jnp.int32, (x.shape[0], TOP_K), 1)
    idx_ref[...] = jnp.where(slot == 0, i1, i2)
    w_ref[...] = jnp.where(slot == 0, 1.0 / den, e2 / den)


def _router(x2, nw, router):
    t, d = x2.shape
    tm = ROUTER_TM
    assert t % tm == 0
    return pl.pallas_call(
        _router_kernel,
        grid=(t // tm,),
        in_specs=[pl.BlockSpec((tm, d), lambda i: (i, 0)),
                  _const_spec(nw.shape), _const_spec(router.shape)],
        out_specs=[pl.BlockSpec((tm, d), lambda i: (i, 0)),
                   pl.BlockSpec((tm, TOP_K), lambda i: (i, 0)),
                   pl.BlockSpec((tm, TOP_K), lambda i: (i, 0))],
        out_shape=[jax.ShapeDtypeStruct((t, d), BF16),
                   jax.ShapeDtypeStruct((t, TOP_K), jnp.int32),
                   jax.ShapeDtypeStruct((t, TOP_K), F32)],
        compiler_params=pltpu.CompilerParams(
            dimension_semantics=("arbitrary",), vmem_limit_bytes=VMEM_LIMIT),
        name="moe_router",
    )(x2, nw, router)


def _route(top_i, n_experts, bm, tc):
    t = top_i.shape[0]
    ex = jnp.arange(n_experts, dtype=jnp.int32)
    sel = top_i[:, :, None] == ex[None, None, :]
    chosen = jnp.any(sel, axis=1).astype(jnp.int32)
    csum = jnp.cumsum(chosen, axis=0)
    counts = csum[-1]
    nb_e = (counts + bm - 1) // bm
    blk_end = jnp.cumsum(nb_e)
    blk_start = blk_end - nb_e
    row0 = blk_start * bm
    seli = sel.astype(jnp.int32)
    pos = jnp.sum(seli * (csum - 1 + row0[None, :])[:, None, :], axis=2)

    nb = (t * TOP_K) // bm + n_experts
    bi = jnp.arange(nb, dtype=jnp.int32)
    blk_e = jnp.minimum(jnp.sum((bi[:, None] >= blk_end[None, :]).astype(jnp.int32), axis=1),
                        n_experts - 1)
    nblk = blk_end[-1]
    active = bi < nblk

    chunk_end = csum[tc - 1::tc]
    chunk_prev = jnp.concatenate([jnp.zeros((1, n_experts), jnp.int32), chunk_end[:-1]], axis=0)
    o_lo = (bi - blk_start[blk_e]) * bm
    o_hi = jnp.minimum(o_lo + bm, counts[blk_e])
    ce_blk = chunk_end.T[blk_e]
    c_lo = jnp.sum((ce_blk <= o_lo[:, None]).astype(jnp.int32), axis=1)
    c_hi = jnp.sum((ce_blk < o_hi[:, None]).astype(jnp.int32), axis=1)
    n_ch = jnp.where(active & (o_hi > o_lo), c_hi - c_lo + 1, 0)
    c_lo = jnp.where(n_ch > 0, c_lo, 0)

    pair_row = row0[None, :] + chunk_prev
    pair_cnt = chunk_end - chunk_prev
    i32 = lambda a: a.astype(jnp.int32)
    return (i32(pos), i32(blk_e), i32(nblk).reshape(1), i32(c_lo), i32(n_ch), i32(o_lo),
            i32(chunk_end), i32(pair_row), i32(pair_cnt))


def _chunk_copies(h_hbm, pos_hbm, hbuf, pbuf, sem, chunk, slot):
    return (pltpu.make_async_copy(h_hbm.at[chunk], hbuf.at[slot], sem.at[0, slot]),
            pltpu.make_async_copy(pos_hbm.at[chunk], pbuf.at[slot], sem.at[1, slot]))


def _gather_kernel(c_lo_ref, n_ch_ref, h_hbm, pos_hbm, o_ref, hbuf, pbuf, acc, sem):
    i = pl.program_id(0)
    bm = o_ref.shape[0]
    tc = hbuf.shape[1]
    c0 = c_lo_ref[i]
    n = n_ch_ref[i]
    acc[...] = jnp.zeros_like(acc)

    @pl.when(n > 0)
    def _():
        for cp in _chunk_copies(h_hbm, pos_hbm, hbuf, pbuf, sem, c0, 0):
            cp.start()

    base = i * bm
    rows = lax.broadcasted_iota(jnp.int32, (bm, 1), 0) + base

    def body(j, carry):
        slot = j % 2

        @pl.when(j + 1 < n)
        def _():
            for cp in _chunk_copies(h_hbm, pos_hbm, hbuf, pbuf, sem, c0 + j + 1, 1 - slot):
                cp.start()

        for cp in _chunk_copies(h_hbm, pos_hbm, hbuf, pbuf, sem, c0 + j, slot):
            cp.wait()
        pc = pbuf[slot]
        hit = rows == pc[0:1, :]
        for kk in range(1, TOP_K):
            hit = jnp.logical_or(hit, rows == pc[kk:kk + 1, :])
        onehot = jnp.where(hit, 1.0, 0.0).astype(BF16)
        acc[...] += _dot(onehot, hbuf[slot])
        return carry

    lax.fori_loop(0, n, body, 0)
    o_ref[...] = acc[...].astype(o_ref.dtype)


def _moe_gather(h, pos, c_lo, n_ch, nb, bm, tc):
    t, d = h.shape
    nc = t // tc
    h3 = h.reshape(nc, tc, d)
    pos3 = pos.reshape(nc, tc, TOP_K).transpose(0, 2, 1)
    grid_spec = pltpu.PrefetchScalarGridSpec(
        num_scalar_prefetch=2,
        grid=(nb,),
        in_specs=[pl.BlockSpec(memory_space=pl.ANY), pl.BlockSpec(memory_space=pl.ANY)],
        out_specs=pl.BlockSpec((bm, d), lambda i, a, b: (i, 0)),
        scratch_shapes=[pltpu.VMEM((2, tc, d), BF16),
                        pltpu.VMEM((2, TOP_K, tc), jnp.int32),
                        pltpu.VMEM((bm, d), F32),
                        pltpu.SemaphoreType.DMA((2, 2))],
    )
    return pl.pallas_call(
        _gather_kernel,
        grid_spec=grid_spec,
        out_shape=jax.ShapeDtypeStruct((nb * bm, d), BF16),
        compiler_params=pltpu.CompilerParams(
            dimension_semantics=("arbitrary",), vmem_limit_bytes=VMEM_LIMIT),
        name="moe_gather",
    )(c_lo, n_ch, h3, pos3)


def _experts_kernel(blk_e_ref, nblk_ref, x_ref, wg_ref, wu_ref, wd_ref, o_ref, acc):
    del blk_e_ref
    i = pl.program_id(0)
    f = pl.program_id(1)
    nf = pl.num_programs(1)
    active = i < nblk_ref[0]

    @pl.when(active)
    def _():
        xb = x_ref[...]
        gt = _dot(xb, wg_ref[0])
        up = _dot(xb, wu_ref[0])
        a = (gt * _sigmoid(gt) * up).astype(BF16)
        part = _dot(a, wd_ref[0])

        @pl.when(f == 0)
        def _():
            acc[...] = part

        @pl.when(f > 0)
        def _():
            acc[...] = acc[...] + part

    @pl.when(f == nf - 1)
    def _():
        @pl.when(active)
        def _():
            o_ref[...] = acc[...].astype(o_ref.dtype)

        @pl.when(jnp.logical_not(active))
        def _():
            o_ref[...] = jnp.zeros_like(o_ref)


def _moe_experts(x_sorted, blk_e, nblk, wg, wu, wd, bm):
    rows, d = x_sorted.shape
    nb = rows // bm
    tf = MOE_TF
    ff = wg.shape[2]
    assert ff % tf == 0
    nf = ff // tf
    grid_spec = pltpu.PrefetchScalarGridSpec(
        num_scalar_prefetch=2,
        grid=(nb, nf),
        in_specs=[
            pl.BlockSpec((bm, d), lambda i, f, be, nk: (i, 0)),
            pl.BlockSpec((1, d, tf), lambda i, f, be, nk: (be[i], 0, f)),
            pl.BlockSpec((1, d, tf), lambda i, f, be, nk: (be[i], 0, f)),
            pl.BlockSpec((1, tf, d), lambda i, f, be, nk: (be[i], f, 0)),
        ],
        out_specs=pl.BlockSpec((bm, d), lambda i, f, be, nk: (i, 0)),
        scratch_shapes=[pltpu.VMEM((bm, d), F32)],
    )
    return pl.pallas_call(
        _experts_kernel,
        grid_spec=grid_spec,
        out_shape=jax.ShapeDtypeStruct((rows, d), BF16),
        compiler_params=pltpu.CompilerParams(
            dimension_semantics=("arbitrary", "arbitrary"),
            vmem_limit_bytes=VMEM_LIMIT),
        name="moe_experts",
    )(blk_e, nblk, x_sorted, wg, wu, wd)


def _window_start(row, n_rows):
    a0 = (row // BF16_SUBLANES) * BF16_SUBLANES
    return pl.multiple_of(jnp.minimum(a0, n_rows - COMBINE_WIN), BF16_SUBLANES)


def _window_copy(y_hbm, ybuf, sem, start, slot, e):
    return pltpu.make_async_copy(y_hbm.at[pl.ds(start, COMBINE_WIN), :],
                                 ybuf.at[slot, pl.ds(e * COMBINE_WIN, COMBINE_WIN), :],
                                 sem.at[slot, e])


def _combine_kernel(row_ref, cnt_ref, x_ref, idx_ref, w_ref, pos_ref, nw_ref, y_hbm,
                    o_ref, ybuf, mbuf, xtra, zacc, sem, xsem, *, apply_norm):
    j = pl.program_id(0)
    nt = pl.num_programs(0)
    ne = mbuf.shape[1] // COMBINE_WIN
    n_rows = y_hbm.shape[0]
    slot = j % 2

    def start_tile(tile, s):
        for e in range(ne):
            _window_copy(y_hbm, ybuf, sem, _window_start(row_ref[tile * ne + e], n_rows), s, e).start()

    @pl.when(j == 0)
    def _():
        start_tile(0, 0)

    @pl.when(j + 1 < nt)
    def _():
        start_tile(j + 1, 1 - slot)

    idx = idx_ref[...]
    wts = w_ref[...]
    pos = pos_ref[...]
    lanes = lax.broadcasted_iota(jnp.int32, (1, COMBINE_WIN), 1)

    def weights_tile(e, start, skip):
        m = jnp.zeros((idx.shape[0], COMBINE_WIN), F32)
        for kk in range(TOP_K):
            off = pos[:, kk:kk + 1] - start
            valid = jnp.logical_and(idx[:, kk:kk + 1] == e, off >= skip)
            off = jnp.where(valid, off, -1)
            m = jnp.where(off == lanes, wts[:, kk:kk + 1], m)
        return m.astype(BF16)

    def finish(s):
        starts = [_window_start(row_ref[j * ne + e], n_rows) for e in range(ne)]
        for e in range(ne):
            mbuf[:, e * COMBINE_WIN:(e + 1) * COMBINE_WIN] = weights_tile(e, starts[e], 0)
        for e in range(ne):
            _window_copy(y_hbm, ybuf, sem, starts[e], s, e).wait()
        z = x_ref[...] + _dot(mbuf[...], ybuf[s])
        n_more = [jnp.maximum((row_ref[j * ne + e] - starts[e] + cnt_ref[j * ne + e]
                               + COMBINE_WIN - 1) // COMBINE_WIN - 1, 0) for e in range(ne)]
        any_more = functools.reduce(lambda a, b: a + b, n_more)

        def emit(zz):
            o_ref[...] = _rmsnorm(zz, nw_ref[...]) if apply_norm else zz

        @pl.when(any_more == 0)
        def _():
            emit(z)

        @pl.when(any_more > 0)
        def _():
            zacc[...] = z
            for e in range(ne):
                def more(m, carry, e=e, start=starts[e]):
                    want = start + (m + 1) * COMBINE_WIN
                    st = pl.multiple_of(jnp.minimum(want, n_rows - COMBINE_WIN), BF16_SUBLANES)
                    cp = pltpu.make_async_copy(y_hbm.at[pl.ds(st, COMBINE_WIN), :], xtra, xsem.at[0])
                    cp.start()
                    cp.wait()
                    zacc[...] += _dot(weights_tile(e, st, want - st), xtra[...])
                    return carry

                lax.fori_loop(0, n_more[e], more, 0)
            emit(zacc[...])

    finish(slot)


def _moe_combine(x2, y_sorted, top_i, top_w, pos, pair_row, pair_cnt, nw, apply_norm):
    t, d = x2.shape
    tm = MOE_TC
    nt = t // tm
    ne = pair_row.shape[1]
    assert y_sorted.shape[0] >= COMBINE_WIN
    grid_spec = pltpu.PrefetchScalarGridSpec(
        num_scalar_prefetch=2,
        grid=(nt,),
        in_specs=[
            pl.BlockSpec((tm, d), lambda i, a, b: (i, 0)),
            pl.BlockSpec((tm, TOP_K), lambda i, a, b: (i, 0)),
            pl.BlockSpec((tm, TOP_K), lambda i, a, b: (i, 0)),
            pl.BlockSpec((tm, TOP_K), lambda i, a, b: (i, 0)),
            pl.BlockSpec(nw.shape, lambda i, a, b: (0, 0)),
            pl.BlockSpec(memory_space=pl.ANY),
        ],
        out_specs=pl.BlockSpec((tm, d), lambda i, a, b: (i, 0)),
        scratch_shapes=[pltpu.VMEM((2, ne * COMBINE_WIN, d), BF16),
                        pltpu.VMEM((tm, ne * COMBINE_WIN), BF16),
                        pltpu.VMEM((COMBINE_WIN, d), BF16),
                        pltpu.VMEM((tm, d), F32),
                        pltpu.SemaphoreType.DMA((2, ne)),
                        pltpu.SemaphoreType.DMA((1,))],
    )
    return pl.pallas_call(
        functools.partial(_combine_kernel, apply_norm=apply_norm),
        grid_spec=grid_spec,
        out_shape=jax.ShapeDtypeStruct((t, d), F32),
        compiler_params=pltpu.CompilerParams(
            dimension_semantics=("arbitrary",), vmem_limit_bytes=VMEM_LIMIT),
        name="moe_combine",
    )(pair_row.reshape(-1), pair_cnt.reshape(-1), x2, top_i, top_w, pos, nw, y_sorted)


def _final_norm_kernel(x_ref, nw_ref, o_ref):
    o_ref[...] = _rmsnorm(x_ref[...], nw_ref[...])


def _final_norm(x2, nw):
    t, d = x2.shape
    tm = DENSE_TM
    return pl.pallas_call(
        _final_norm_kernel,
        grid=(t // tm,),
        in_specs=[pl.BlockSpec((tm, d), lambda i: (i, 0)), _const_spec(nw.shape)],
        out_specs=pl.BlockSpec((tm, d), lambda i: (i, 0)),
        out_shape=jax.ShapeDtypeStruct((t, d), F32),
        name="final_norm",
    )(x2, nw)


def kernel(x, attn_norm_w, w_in, gate_bias, pool_w, pool_scale, pool_up, hgrn_lower_bounds,
           hgrn_norm_w, hgrn_up, w_out, ffn_norm_w, dense_w_gate, dense_w_up, dense_w_down,
           moe_router, moe_w_gate, moe_w_up, moe_w_down, final_norm_w):
    b, s, d = x.shape
    depth = w_in.shape[0]
    n_experts = moe_router.shape[-1]

    lb_soft = jax.nn.softmax(hgrn_lower_bounds.astype(F32), axis=0)
    lb = jnp.clip(jnp.cumsum(lb_soft, axis=0) - lb_soft[0:1], 0.0, 1.0 - 1e-6)
    lbf = jnp.maximum(lb, LB_FLOOR)
    oml = 1.0 - lb
    dlb = lb - lbf

    def row(a):
        return a.reshape(1, -1).astype(F32)

    final_w = row(final_norm_w)
    normed = None
    for l in range(depth):
        x = _token_mixer(
            x, row(attn_norm_w[l]), w_in[l].astype(BF16), row(gate_bias[l]),
            pool_w[l].astype(BF16), row(pool_scale[l]), pool_up[l].astype(BF16),
            row(lbf[l]), row(oml[l]), row(dlb[l]),
            row(jnp.tile(hgrn_norm_w[l], HGRN_HEADS)), hgrn_up[l].astype(BF16),
            w_out[l].astype(BF16))
        x2 = x.reshape(b * s, d)
        j = l // 2
        last = l == depth - 1
        if l % 2 == 0:
            x2 = _dense_ffn(x2, row(ffn_norm_w[l]), dense_w_gate[j].astype(BF16),
                            dense_w_up[j].astype(BF16), dense_w_down[j].astype(BF16))
            if last:
                normed = _final_norm(x2, final_w)
        else:
            h, top_i, top_w = _router(x2, row(ffn_norm_w[l]), moe_router[j].astype(F32))
            pos, blk_e, nblk, c_lo, n_ch, _, _, pair_row, pair_cnt = _route(
                top_i, n_experts, MOE_BM, MOE_TC)
            x_sorted = _moe_gather(h, pos, c_lo, n_ch, blk_e.shape[0], MOE_BM, MOE_TC)
            y_sorted = _moe_experts(x_sorted, blk_e, nblk, moe_w_gate[j].astype(BF16),
                                    moe_w_up[j].astype(BF16), moe_w_down[j].astype(BF16), MOE_BM)
            x2 = _moe_combine(x2, y_sorted, top_i, top_w, pos, pair_row, pair_cnt, final_w,
                              apply_norm=last)
            if last:
                normed = x2
        x = x2.reshape(b, s, d)
    return normed.reshape(b, s, d)
```

```python
import functools

import jax
import jax.numpy as jnp
from jax import lax
from jax.experimental import pallas as pl
from jax.experimental.pallas import tpu as pltpu

F32 = jnp.float32
BF16 = jnp.bfloat16

EPS = 1e-6
LB_FLOOR = 1e-30
POOL_WINDOWS = (2, 4, 8, 16)
N_POOL_GROUPS = 4
HGRN_HEADS = 4
TOP_K = 2
LANE = 128
BF16_SUBLANES = 16

MIXER_TM = 512
HGRN_CHUNK = 256
POOL_CARRY = 16
DENSE_TM = 512
ROUTER_TM = 512
MOE_BM = 512
MOE_TF = 1792
MOE_TC = 512
DISPATCH_WIN = 256
COMBINE_WIN = 256
VMEM_LIMIT = 56 * 1024 * 1024


def _const_spec(shape):
    nd = len(shape)
    return pl.BlockSpec(shape, lambda *_: (0,) * nd, pipeline_mode=pl.Buffered(1))


def _dot(a, b):
    return jnp.dot(a, b, preferred_element_type=F32)


def _dot_nt(a, b):
    return lax.dot_general(a, b, (((1,), (1,)), ((), ())), preferred_element_type=F32)


def _dot_tn(a, b):
    return lax.dot_general(a, b, (((0,), (0,)), ((), ())), preferred_element_type=F32)


def _sigmoid(x):
    return 1.0 / (1.0 + jnp.exp(-x))


def _rmsnorm(x, w):
    ms = jnp.mean(x * x, axis=-1, keepdims=True)
    return x * lax.rsqrt(ms + EPS) * w


def _hgrn_chunk(q, k, v, c, state_ref, hd):
    n = q.shape[0]
    rows = lax.broadcasted_iota(jnp.int32, (n, 1), 0)
    cols = lax.broadcasted_iota(jnp.int32, (1, n), 1)
    v16 = v.astype(BF16)
    scores = [jnp.zeros((n, n), F32) for _ in range(HGRN_HEADS)]

    p = c
    tot = c
    half = 1
    shift = 1
    while half < n:
        upper = (rows & half) != 0
        e = jnp.exp(jnp.where(upper, p, tot - p))
        same_block = (rows >> shift) == (cols >> shift)
        for hi in range(HGRN_HEADS):
            sl = slice(hi * hd, (hi + 1) * hd)
            eh = e[:, sl]
            qt = jnp.where(upper, q[:, sl] * eh, 0.0).astype(BF16)
            kt = jnp.where(upper, 0.0, k[:, sl] * eh).astype(BF16)
            scores[hi] = scores[hi] + jnp.where(same_block, _dot_nt(qt, kt), 0.0)
        other = jnp.where(upper, pltpu.roll(tot, half, 0), pltpu.roll(tot, n - half, 0))
        p = p + jnp.where(upper, other, 0.0)
        tot = tot + other
        half *= 2
        shift += 1

    eg = jnp.exp(p)
    er = jnp.exp(tot - p)
    o_heads = []
    for hi in range(HGRN_HEADS):
        sl = slice(hi * hd, (hi + 1) * hd)
        qh, kh, vh = q[:, sl], k[:, sl], v[:, sl]
        st = state_ref[hi]
        o = _dot(scores[hi].astype(BF16), v16[:, sl])
        o = o + jnp.sum(qh * kh, axis=-1, keepdims=True) * vh
        o = o + _dot_nt((qh * eg[:, sl]).astype(BF16), st.astype(BF16))
        kd = (kh * er[:, sl]).astype(BF16)
        state_ref[hi] = st * jnp.exp(tot[0:1, sl]) + _dot_tn(v16[:, sl], kd)
        o = o * lax.rsqrt(jnp.mean(o * o, axis=-1, keepdims=True) + EPS)
        o_heads.append(o)
    return jnp.concatenate(o_heads, axis=1)


def _mixer_kernel(x_ref, nw_ref, win_ref, gb_ref, pw_ref, ps_ref, pup_ref,
                  lbf_ref, oml_ref, dl_ref, hnw_ref, hup_ref, wout_ref,
                  o_ref, carry_ref, state_ref):
    tm = x_ref.shape[1]
    d = x_ref.shape[2]
    pw_cols = pup_ref.shape[0]
    hw = hup_ref.shape[0]
    gd = pw_cols // N_POOL_GROUPS
    hd = hw // HGRN_HEADS
    off_q = pw_cols
    off_f = off_q + hw
    off_i = off_f + hw
    off_g = off_i + hw
    off_gates = off_g + hw
    si = pl.program_id(1)

    @pl.when(si == 0)
    def _():
        carry_ref[...] = jnp.zeros_like(carry_ref)
        state_ref[...] = jnp.zeros_like(state_ref)

    x = x_ref[0]
    h = _rmsnorm(x, nw_ref[...]).astype(BF16)

    u = _dot(h, win_ref[:, 0:pw_cols])
    ext = jnp.concatenate([carry_ref[...], u], axis=0)
    carry_ref[...] = u[tm - POOL_CARRY:, :]
    pos = (si * tm + 1 + lax.broadcasted_iota(jnp.int32, (tm, 1), 0)).astype(F32)
    acc = ext
    span = 1
    mixed = []
    for gi, w in enumerate(POOL_WINDOWS):
        while span < w:
            acc = acc + pltpu.roll(acc, span, 0)
            span *= 2
        wsum = acc[POOL_CARRY:, 0:gd]
        acc = acc[:, gd:] if gi + 1 < N_POOL_GROUPS else None
        ug = u[:, gi * gd:(gi + 1) * gd]
        pooled = wsum * (1.0 / jnp.minimum(pos, float(w))) - ug
        mixed.append(_dot(pooled.astype(BF16), pw_ref[gi]))
    mixed = jnp.concatenate(mixed, axis=1) * ps_ref[...]
    y_pool = _dot(mixed.astype(BF16), pup_ref[...])

    q_pre = _dot(h, win_ref[:, off_q:off_f])
    q = q_pre * _sigmoid(q_pre)
    fp = _dot(h, win_ref[:, off_f:off_i])
    v = _dot(h, win_ref[:, off_i:off_g])
    g_pre = _dot(h, win_ref[:, off_g:off_gates])
    g = g_pre * _sigmoid(g_pre)

    en = jnp.exp(-jnp.abs(fp))
    rden = 1.0 / (1.0 + en)
    nonneg = fp >= 0
    spos = jnp.where(nonneg, 1.0, en) * rden
    sneg = jnp.where(nonneg, en, 1.0) * rden
    c = jnp.log(lbf_ref[...] + oml_ref[...] * spos)
    k = oml_ref[...] * sneg + dl_ref[...]

    ck = HGRN_CHUNK
    o_chunks = [_hgrn_chunk(q[r0:r0 + ck], k[r0:r0 + ck], v[r0:r0 + ck], c[r0:r0 + ck],
                            state_ref, hd)
                for r0 in range(0, tm, ck)]
    o_all = jnp.concatenate(o_chunks, axis=0) * hnw_ref[...] * g
    y_hgrn = _dot(o_all.astype(BF16), hup_ref[...])

    gz = _dot(h, win_ref[:, off_gates:off_gates + 2 * d]) + gb_ref[...]
    gates = _sigmoid(gz)
    merged = gates[:, 0:d] * y_pool + gates[:, d:2 * d] * y_hgrn
    o_ref[0] = x + _dot(merged.astype(BF16), wout_ref[...])


def _token_mixer(x, nw, win, gb, pw, ps, pup, lbf, oml, dl, hnw, hup, wout):
    b, s, d = x.shape
    tm = MIXER_TM
    pw_cols = pup.shape[0]
    hw = hup.shape[0]
    hd = hw // HGRN_HEADS
    assert s % tm == 0 and tm >= POOL_CARRY
    consts = (nw, win, gb, pw, ps, pup, lbf, oml, dl, hnw, hup, wout)
    return pl.pallas_call(
        _mixer_kernel,
        grid=(b, s // tm),
        in_specs=[pl.BlockSpec((1, tm, d), lambda bi, si: (bi, si, 0))]
        + [_const_spec(a.shape) for a in consts],
        out_specs=pl.BlockSpec((1, tm, d), lambda bi, si: (bi, si, 0)),
        out_shape=jax.ShapeDtypeStruct(x.shape, F32),
        scratch_shapes=[pltpu.VMEM((POOL_CARRY, pw_cols), F32),
                        pltpu.VMEM((HGRN_HEADS, hd, hd), F32)],
        compiler_params=pltpu.CompilerParams(
            dimension_semantics=("arbitrary", "arbitrary"),
            vmem_limit_bytes=VMEM_LIMIT),
        name="token_mixer",
    )(x, *consts)


def _dense_ffn_kernel(x_ref, nw_ref, wg_ref, wu_ref, wd_ref, o_ref, *, n_chunks):
    x = x_ref[...]
    h = _rmsnorm(x, nw_ref[...]).astype(BF16)
    ff = wg_ref.shape[1]
    fc = ff // n_chunks
    out = x
    for ci in range(n_chunks):
        sl = slice(ci * fc, (ci + 1) * fc)
        gt = _dot(h, wg_ref[:, sl])
        up = _dot(h, wu_ref[:, sl])
        a = (gt * _sigmoid(gt) * up).astype(BF16)
        out = out + _dot(a, wd_ref[sl, :])
    o_ref[...] = out


def _dense_ffn(x2, nw, wg, wu, wd):
    t, d = x2.shape
    tm = DENSE_TM
    ff = wg.shape[1]
    n_chunks = 2 if ff % (2 * LANE) == 0 else 1
    assert t % tm == 0
    return pl.pallas_call(
        functools.partial(_dense_ffn_kernel, n_chunks=n_chunks),
        grid=(t // tm,),
        in_specs=[pl.BlockSpec((tm, d), lambda i: (i, 0)),
                  _const_spec(nw.shape), _const_spec(wg.shape),
                  _const_spec(wu.shape), _const_spec(wd.shape)],
        out_specs=pl.BlockSpec((tm, d), lambda i: (i, 0)),
        out_shape=jax.ShapeDtypeStruct(x2.shape, F32),
        compiler_params=pltpu.CompilerParams(
            dimension_semantics=("arbitrary",), vmem_limit_bytes=VMEM_LIMIT),
        name="dense_ffn",
    )(x2, nw, wg, wu, wd)


def _router_kernel(x_ref, nw_ref, r_ref, h_ref, idx_ref, w_ref):
    x = x_ref[...]
    h = _rmsnorm(x, nw_ref[...])
    h_ref[...] = h.astype(BF16)
    logits = lax.dot_general(h, r_ref[...], (((1,), (0,)), ((), ())),
                             precision=lax.Precision.HIGHEST,
                             preferred_element_type=F32)
    ne = logits.shape[1]
    lane = lax.broadcasted_iota(jnp.int32, logits.shape, 1)
    m1 = jnp.max(logits, axis=-1, keepdims=True)
    i1 = jnp.min(jnp.where(logits == m1, lane, ne), axis=-1, keepdims=True)
    rest = jnp.where(lane == i1, -jnp.inf, logits)
    m2 = jnp.max(rest, axis=-1, keepdims=True)
    i2 = jnp.min(jnp.where(rest == m2, lane, ne), axis=-1, keepdims=True)
    e2 = jnp.exp(m2 - m1)
    den = 1.0 + e2
    slot = lax.broadcasted_iota(jnp.int32, (x.shape[0], TOP_K), 1)
    idx_ref[...] = jnp.where(slot == 0, i1, i2)
    w_ref[...] = jnp.where(slot == 0, 1.0 / den, e2 / den)


def _router(x2, nw, router):
    t, d = x2.shape
    tm = ROUTER_TM
    assert t % tm == 0
    return pl.pallas_call(
        _router_kernel,
        grid=(t // tm,),
        in_specs=[pl.BlockSpec((tm, d), lambda i: (i, 0)),
                  _const_spec(nw.shape), _const_spec(router.shape)],
        out_specs=[pl.BlockSpec((tm, d), lambda i: (i, 0)),
                   pl.BlockSpec((tm, TOP_K), lambda i: (i, 0)),
                   pl.BlockSpec((tm, TOP_K), lambda i: (i, 0))],
        out_shape=[jax.ShapeDtypeStruct((t, d), BF16),
                   jax.ShapeDtypeStruct((t, TOP_K), jnp.int32),
                   jax.ShapeDtypeStruct((t, TOP_K), F32)],
        compiler_params=pltpu.CompilerParams(
            dimension_semantics=("arbitrary",), vmem_limit_bytes=VMEM_LIMIT),
        name="moe_router",
    )(x2, nw, router)


def _route(top_i, n_experts, bm, tc):
    t = top_i.shape[0]
    ex = jnp.arange(n_experts, dtype=jnp.int32)
    sel = top_i[:, :, None] == ex[None, None, :]
    chosen = jnp.any(sel, axis=1).astype(jnp.int32)
    csum = jnp.cumsum(chosen, axis=0)
    counts = csum[-1]
    nb_e = (counts + bm - 1) // bm + 1
    blk_end = jnp.cumsum(nb_e)
    blk_start = blk_end - nb_e
    row0 = blk_start * bm
    seli = sel.astype(jnp.int32)
    pos = jnp.sum(seli * (csum - 1 + row0[None, :])[:, None, :], axis=2)

    nb = (t * TOP_K) // bm + 2 * n_experts
    bi = jnp.arange(nb, dtype=jnp.int32)
    blk_e = jnp.minimum(jnp.sum((bi[:, None] >= blk_end[None, :]).astype(jnp.int32), axis=1),
                        n_experts - 1)
    blk_valid = jnp.clip(counts[blk_e] - (bi - blk_start[blk_e]) * bm, 0, bm)
    blk_valid = jnp.where(bi < blk_end[-1], blk_valid, 0)

    tile_end = csum[tc - 1::tc]
    tile_prev = jnp.concatenate([jnp.zeros((1, n_experts), jnp.int32), tile_end[:-1]], axis=0)
    pair_row = row0[None, :] + tile_prev
    pair_cnt = tile_end - tile_prev
    i32 = lambda a: a.astype(jnp.int32)
    return i32(pos), i32(blk_e), i32(blk_valid), i32(pair_row), i32(pair_cnt)


def _dispatch_window_write(wbuf, x_hbm, sem, start, slot, e):
    return pltpu.make_async_copy(wbuf.at[slot, e], x_hbm.at[pl.ds(start, DISPATCH_WIN), :],
                                 sem.at[slot, e])


def _dispatch_kernel(row_ref, cnt_ref, h_ref, idx_ref, pos_ref, zeros_hbm, x_hbm,
                     wbuf, xbuf, carry, sem, xsem):
    del zeros_hbm
    j = pl.program_id(0)
    nt = pl.num_programs(0)
    ne = carry.shape[0]
    win = DISPATCH_WIN
    grp = BF16_SUBLANES
    slot = j % 2

    @pl.when(j == 0)
    def _():
        carry[...] = jnp.zeros_like(carry)

    @pl.when(j > 0)
    def _():
        for e in range(ne):
            _dispatch_window_write(wbuf, x_hbm, sem, 0, 1 - slot, e).wait()

    h = h_ref[...]
    idx = idx_ref[0]
    pos = pos_ref[0]
    rows = lax.broadcasted_iota(jnp.int32, (win, 1), 0)

    def selection(e, start):
        hit = None
        for kk in range(TOP_K):
            off = jnp.where(idx[kk:kk + 1, :] == e, pos[kk:kk + 1, :] - start, -1)
            hk = off == rows
            hit = hk if hit is None else jnp.logical_or(hit, hk)
        return jnp.where(hit, 1.0, 0.0).astype(BF16)

    for e in range(ne):
        row = row_ref[j * ne + e]
        cnt = cnt_ref[j * ne + e]
        a0 = pl.multiple_of((row // grp) * grp, grp)
        fill = row - a0 + cnt
        w = _dot(selection(e, a0), h)
        head = w[0:grp, :] + carry[e].astype(F32)
        wbuf[slot, e, 0:grp, :] = head.astype(BF16)
        wbuf[slot, e, grp:, :] = w[grp:, :].astype(BF16)
        _dispatch_window_write(wbuf, x_hbm, sem, a0, slot, e).start()
        n_more = fill // win

        @pl.when(n_more == 0)
        def _(e=e, fill=fill):
            g = pl.multiple_of((fill // grp) * grp, grp)
            carry[e] = wbuf[slot, e, pl.ds(g, grp), :]

        @pl.when(n_more > 0)
        def _(e=e, a0=a0, fill=fill, n_more=n_more):
            def more(m, c):
                st = pl.multiple_of(a0 + (m + 1) * win, grp)
                xbuf[...] = _dot(selection(e, st), h).astype(BF16)
                cp = pltpu.make_async_copy(xbuf, x_hbm.at[pl.ds(st, win), :], xsem.at[0])
                cp.start()
                cp.wait()
                return c

            lax.fori_loop(0, n_more, more, 0)
            g = pl.multiple_of(((fill - n_more * win) // grp) * grp, grp)
            carry[e] = xbuf[pl.ds(g, grp), :]

    @pl.when(j == nt - 1)
    def _():
        for e in range(ne):
            _dispatch_window_write(wbuf, x_hbm, sem, 0, slot, e).wait()


def _moe_dispatch(h, top_i, pos, pair_row, pair_cnt, n_rows, tc):
    t, d = h.shape
    nt = t // tc
    ne = pair_row.shape[1]
    lane_major = lambda a: a.reshape(nt, tc, TOP_K).transpose(0, 2, 1)
    grid_spec = pltpu.PrefetchScalarGridSpec(
        num_scalar_prefetch=2,
        grid=(nt,),
        in_specs=[pl.BlockSpec((tc, d), lambda j, a, b: (j, 0)),
                  pl.BlockSpec((1, TOP_K, tc), lambda j, a, b: (j, 0, 0)),
                  pl.BlockSpec((1, TOP_K, tc), lambda j, a, b: (j, 0, 0)),
                  pl.BlockSpec(memory_space=pl.ANY)],
        out_specs=pl.BlockSpec(memory_space=pl.ANY),
        scratch_shapes=[pltpu.VMEM((2, ne, DISPATCH_WIN, d), BF16),
                        pltpu.VMEM((DISPATCH_WIN, d), BF16),
                        pltpu.VMEM((ne, BF16_SUBLANES, d), BF16),
                        pltpu.SemaphoreType.DMA((2, ne)),
                        pltpu.SemaphoreType.DMA((1,))],
    )
    return pl.pallas_call(
        _dispatch_kernel,
        grid_spec=grid_spec,
        out_shape=jax.ShapeDtypeStruct((n_rows, d), BF16),
        input_output_aliases={5: 0},
        compiler_params=pltpu.CompilerParams(
            dimension_semantics=("arbitrary",), vmem_limit_bytes=VMEM_LIMIT),
        name="moe_dispatch",
    )(pair_row.reshape(-1), pair_cnt.reshape(-1), h, lane_major(top_i), lane_major(pos),
      jnp.zeros((n_rows, d), BF16))


def _experts_kernel(blk_e_ref, valid_ref, x_ref, wg_ref, wu_ref, wd_ref, o_ref, acc):
    del blk_e_ref
    i = pl.program_id(0)
    f = pl.program_id(1)
    nf = pl.num_programs(1)
    active = valid_ref[i] > 0

    @pl.when(active)
    def _():
        xb = x_ref[...]
        gt = _dot(xb, wg_ref[0])
        up = _dot(xb, wu_ref[0])
        a = (gt * _sigmoid(gt) * up).astype(BF16)
        part = _dot(a, wd_ref[0])

        @pl.when(f == 0)
        def _():
            acc[...] = part

        @pl.when(f > 0)
        def _():
            acc[...] = acc[...] + part

    @pl.when(f == nf - 1)
    def _():
        @pl.when(active)
        def _():
            o_ref[...] = acc[...].astype(o_ref.dtype)

        @pl.when(jnp.logical_not(active))
        def _():
            o_ref[...] = jnp.zeros_like(o_ref)


def _moe_experts(x_sorted, blk_e, blk_valid, wg, wu, wd, bm):
    rows, d = x_sorted.shape
    nb = rows // bm
    tf = MOE_TF
    ff = wg.shape[2]
    assert ff % tf == 0
    nf = ff // tf

    def wcol(i, f, bv):
        return jnp.where(bv[i] > 0, f, nf - 1)

    grid_spec = pltpu.PrefetchScalarGridSpec(
        num_scalar_prefetch=2,
        grid=(nb, nf),
        in_specs=[
            pl.BlockSpec((bm, d), lambda i, f, be, bv: (i, 0)),
            pl.BlockSpec((1, d, tf), lambda i, f, be, bv: (be[i], 0, wcol(i, f, bv))),
            pl.BlockSpec((1, d, tf), lambda i, f, be, bv: (be[i], 0, wcol(i, f, bv))),
            pl.BlockSpec((1, tf, d), lambda i, f, be, bv: (be[i], wcol(i, f, bv), 0)),
        ],
        out_specs=pl.BlockSpec((bm, d), lambda i, f, be, bv: (i, 0)),
        scratch_shapes=[pltpu.VMEM((bm, d), F32)],
    )
    return pl.pallas_call(
        _experts_kernel,
        grid_spec=grid_spec,
        out_shape=jax.ShapeDtypeStruct((rows, d), BF16),
        compiler_params=pltpu.CompilerParams(
            dimension_semantics=("arbitrary", "arbitrary"),
            vmem_limit_bytes=VMEM_LIMIT),
        name="moe_experts",
    )(blk_e, blk_valid, x_sorted, wg, wu, wd)


def _window_start(row, n_rows):
    a0 = (row // BF16_SUBLANES) * BF16_SUBLANES
    return pl.multiple_of(jnp.minimum(a0, n_rows - COMBINE_WIN), BF16_SUBLANES)


def _window_copy(y_hbm, ybuf, sem, start, slot, e):
    return pltpu.make_async_copy(y_hbm.at[pl.ds(start, COMBINE_WIN), :],
                                 ybuf.at[slot, pl.ds(e * COMBINE_WIN, COMBINE_WIN), :],
                                 sem.at[slot, e])


def _combine_kernel(row_ref, cnt_ref, x_ref, idx_ref, w_ref, pos_ref, nw_ref, y_hbm,
                    o_ref, ybuf, mbuf, xtra, zacc, sem, xsem, *, apply_norm):
    j = pl.program_id(0)
    nt = pl.num_programs(0)
    ne = mbuf.shape[1] // COMBINE_WIN
    n_rows = y_hbm.shape[0]
    slot = j % 2

    def start_tile(tile, s):
        for e in range(ne):
            _window_copy(y_hbm, ybuf, sem, _window_start(row_ref[tile * ne + e], n_rows), s, e).start()

    @pl.when(j == 0)
    def _():
        start_tile(0, 0)

    @pl.when(j + 1 < nt)
    def _():
        start_tile(j + 1, 1 - slot)

    idx = idx_ref[...]
    wts = w_ref[...]
    pos = pos_ref[...]
    lanes = lax.broadcasted_iota(jnp.int32, (1, COMBINE_WIN), 1)

    def weights_tile(e, start, skip):
        m = jnp.zeros((idx.shape[0], COMBINE_WIN), F32)
        for kk in range(TOP_K):
            off = pos[:, kk:kk + 1] - start
            valid = jnp.logical_and(idx[:, kk:kk + 1] == e, off >= skip)
            off = jnp.where(valid, off, -1)
            m = jnp.where(off == lanes, wts[:, kk:kk + 1], m)
        return m.astype(BF16)

    def finish(s):
        starts = [_window_start(row_ref[j * ne + e], n_rows) for e in range(ne)]
        for e in range(ne):
            mbuf[:, e * COMBINE_WIN:(e + 1) * COMBINE_WIN] = weights_tile(e, starts[e], 0)
        for e in range(ne):
            _window_copy(y_hbm, ybuf, sem, starts[e], s, e).wait()
        z = x_ref[...] + _dot(mbuf[...], ybuf[s])
        n_more = [jnp.maximum((row_ref[j * ne + e] - starts[e] + cnt_ref[j * ne + e]
                               + COMBINE_WIN - 1) // COMBINE_WIN - 1, 0) for e in range(ne)]
        any_more = functools.reduce(lambda a, b: a + b, n_more)

        def emit(zz):
            o_ref[...] = _rmsnorm(zz, nw_ref[...]) if apply_norm else zz

        @pl.when(any_more == 0)
        def _():
            emit(z)

        @pl.when(any_more > 0)
        def _():
            zacc[...] = z
            for e in range(ne):
                def more(m, carry, e=e, start=starts[e]):
                    want = start + (m + 1) * COMBINE_WIN
                    st = pl.multiple_of(jnp.minimum(want, n_rows - COMBINE_WIN), BF16_SUBLANES)
                    cp = pltpu.make_async_copy(y_hbm.at[pl.ds(st, COMBINE_WIN), :], xtra, xsem.at[0])
                    cp.start()
                    cp.wait()
                    zacc[...] += _dot(weights_tile(e, st, want - st), xtra[...])
                    return carry

                lax.fori_loop(0, n_more[e], more, 0)
            emit(zacc[...])

    finish(slot)


def _moe_combine(x2, y_sorted, top_i, top_w, pos, pair_row, pair_cnt, nw, apply_norm):
    t, d = x2.shape
    tm = MOE_TC
    nt = t // tm
    ne = pair_row.shape[1]
    assert y_sorted.shape[0] >= COMBINE_WIN
    grid_spec = pltpu.PrefetchScalarGridSpec(
        num_scalar_prefetch=2,
        grid=(nt,),
        in_specs=[
            pl.BlockSpec((tm, d), lambda i, a, b: (i, 0)),
            pl.BlockSpec((tm, TOP_K), lambda i, a, b: (i, 0)),
            pl.BlockSpec((tm, TOP_K), lambda i, a, b: (i, 0)),
            pl.BlockSpec((tm, TOP_K), lambda i, a, b: (i, 0)),
            pl.BlockSpec(nw.shape, lambda i, a, b: (0, 0)),
            pl.BlockSpec(memory_space=pl.ANY),
        ],
        out_specs=pl.BlockSpec((tm, d), lambda i, a, b: (i, 0)),
        scratch_shapes=[pltpu.VMEM((2, ne * COMBINE_WIN, d), BF16),
                        pltpu.VMEM((tm, ne * COMBINE_WIN), BF16),
                        pltpu.VMEM((COMBINE_WIN, d), BF16),
                        pltpu.VMEM((tm, d), F32),
                        pltpu.SemaphoreType.DMA((2, ne)),
                        pltpu.SemaphoreType.DMA((1,))],
    )
    return pl.pallas_call(
        functools.partial(_combine_kernel, apply_norm=apply_norm),
        grid_spec=grid_spec,
        out_shape=jax.ShapeDtypeStruct((t, d), F32),
        compiler_params=pltpu.CompilerParams(
            dimension_semantics=("arbitrary",), vmem_limit_bytes=VMEM_LIMIT),
        name="moe_combine",
    )(pair_row.reshape(-1), pair_cnt.reshape(-1), x2, top_i, top_w, pos, nw, y_sorted)


def _final_norm_kernel(x_ref, nw_ref, o_ref):
    o_ref[...] = _rmsnorm(x_ref[...], nw_ref[...])


def _final_norm(x2, nw):
    t, d = x2.shape
    tm = DENSE_TM
    return pl.pallas_call(
        _final_norm_kernel,
        grid=(t // tm,),
        in_specs=[pl.BlockSpec((tm, d), lambda i: (i, 0)), _const_spec(nw.shape)],
        out_specs=pl.BlockSpec((tm, d), lambda i: (i, 0)),
        out_shape=jax.ShapeDtypeStruct((t, d), F32),
        name="final_norm",
    )(x2, nw)


def kernel(x, attn_norm_w, w_in, gate_bias, pool_w, pool_scale, pool_up, hgrn_lower_bounds,
           hgrn_norm_w, hgrn_up, w_out, ffn_norm_w, dense_w_gate, dense_w_up, dense_w_down,
           moe_router, moe_w_gate, moe_w_up, moe_w_down, final_norm_w):
    b, s, d = x.shape
    depth = w_in.shape[0]
    n_experts = moe_router.shape[-1]

    lb_soft = jax.nn.softmax(hgrn_lower_bounds.astype(F32), axis=0)
    lb = jnp.clip(jnp.cumsum(lb_soft, axis=0) - lb_soft[0:1], 0.0, 1.0 - 1e-6)
    lbf = jnp.maximum(lb, LB_FLOOR)
    oml = 1.0 - lb
    dlb = lb - lbf

    def row(a):
        return a.reshape(1, -1).astype(F32)

    final_w = row(final_norm_w)
    normed = None
    for l in range(depth):
        x = _token_mixer(
            x, row(attn_norm_w[l]), w_in[l].astype(BF16), row(gate_bias[l]),
            pool_w[l].astype(BF16), row(pool_scale[l]), pool_up[l].astype(BF16),
            row(lbf[l]), row(oml[l]), row(dlb[l]),
            row(jnp.tile(hgrn_norm_w[l], HGRN_HEADS)), hgrn_up[l].astype(BF16),
            w_out[l].astype(BF16))
        x2 = x.reshape(b * s, d)
        j = l // 2
        last = l == depth - 1
        if l % 2 == 0:
            x2 = _dense_ffn(x2, row(ffn_norm_w[l]), dense_w_gate[j].astype(BF16),
                            dense_w_up[j].astype(BF16), dense_w_down[j].astype(BF16))
            if last:
                normed = _final_norm(x2, final_w)
        else:
            h, top_i, top_w = _router(x2, row(ffn_norm_w[l]), moe_router[j].astype(F32))
            pos, blk_e, blk_valid, pair_row, pair_cnt = _route(top_i, n_experts, MOE_BM, MOE_TC)
            x_sorted = _moe_dispatch(h, top_i, pos, pair_row, pair_cnt,
                                     blk_e.shape[0] * MOE_BM, MOE_TC)
            y_sorted = _moe_experts(x_sorted, blk_e, blk_valid, moe_w_gate[j].astype(BF16),
                                    moe_w_up[j].astype(BF16), moe_w_down[j].astype(BF16), MOE_BM)
            x2 = _moe_combine(x2, y_sorted, top_i, top_w, pos, pair_row, pair_cnt, final_w,
                              apply_norm=last)
            if last:
                normed = x2
        x = x2.reshape(b, s, d)
    return normed.reshape(b, s, d)
```

```python
import functools

import jax
import jax.numpy as jnp
from jax import lax
from jax.experimental import pallas as pl
from jax.experimental.pallas import tpu as pltpu

F32 = jnp.float32
BF16 = jnp.bfloat16

EPS = 1e-6
LB_FLOOR = 1e-30
POOL_WINDOWS = (2, 4, 8, 16)
N_POOL_GROUPS = 4
HGRN_HEADS = 4
TOP_K = 2
LANE = 128
BF16_SUBLANES = 16

MIXER_TM = 512
HGRN_CHUNK = 256
POOL_CARRY = 16
DENSE_TM = 512
ROUTER_TM = 512
MOE_BM = 512
MOE_TF = 1792
MOE_TC = 512
DISPATCH_WIN = 192
COMBINE_WIN = 256
VMEM_LIMIT = 56 * 1024 * 1024


def _const_spec(shape):
    nd = len(shape)
    return pl.BlockSpec(shape, lambda *_: (0,) * nd, pipeline_mode=pl.Buffered(1))


def _dot(a, b):
    return jnp.dot(a, b, preferred_element_type=F32)


def _dot_nt(a, b):
    return lax.dot_general(a, b, (((1,), (1,)), ((), ())), preferred_element_type=F32)


def _dot_tn(a, b):
    return lax.dot_general(a, b, (((0,), (0,)), ((), ())), preferred_element_type=F32)


def _sigmoid(x):
    return 0.5 * jnp.tanh(0.5 * x) + 0.5


def _rmsnorm(x, w):
    ms = jnp.mean(x * x, axis=-1, keepdims=True)
    return x * lax.rsqrt(ms + EPS) * w


def _hgrn_chunk(q, k, v, c, state_ref, hd, side_work):
    n = q.shape[0]
    rows = lax.broadcasted_iota(jnp.int32, (n, 1), 0)
    cols = lax.broadcasted_iota(jnp.int32, (1, n), 1)
    v16 = v.astype(BF16)
    scores = [jnp.zeros((n, n), F32) for _ in range(HGRN_HEADS)]

    p = c
    tot = c
    half = 1
    shift = 1
    while half < n:
        upper = (rows & half) != 0
        e = jnp.exp2(jnp.where(upper, p, tot - p))
        row_key = jnp.where(upper, rows >> shift, -1)
        col_key = jnp.where((cols & half) != 0, -2, cols >> shift)
        owned = row_key == col_key
        for hi in range(HGRN_HEADS):
            sl = slice(hi * hd, (hi + 1) * hd)
            eh = e[:, sl]
            qt = (q[:, sl] * eh).astype(BF16)
            kt = (k[:, sl] * eh).astype(BF16)
            scores[hi] = jnp.where(owned, _dot_nt(qt, kt), scores[hi])
        other = jnp.where(upper, pltpu.roll(tot, half, 0), pltpu.roll(tot, n - half, 0))
        p = p + jnp.where(upper, other, 0.0)
        tot = tot + other
        if side_work:
            side_work.pop(0)()
        half *= 2
        shift += 1

    eg = jnp.exp2(p)
    er = jnp.exp2(tot - p)
    o_heads = []
    for hi in range(HGRN_HEADS):
        sl = slice(hi * hd, (hi + 1) * hd)
        qh, kh, vh = q[:, sl], k[:, sl], v[:, sl]
        st = state_ref[hi]
        o = _dot(scores[hi].astype(BF16), v16[:, sl])
        o = o + jnp.sum(qh * kh, axis=-1, keepdims=True) * vh
        o = o + _dot_nt((qh * eg[:, sl]).astype(BF16), st.astype(BF16))
        kd = (kh * er[:, sl]).astype(BF16)
        state_ref[hi] = st * jnp.exp2(tot[0:1, sl]) + _dot_tn(v16[:, sl], kd)
        o = o * lax.rsqrt(jnp.mean(o * o, axis=-1, keepdims=True) + EPS)
        o_heads.append(o)
    return jnp.concatenate(o_heads, axis=1)


def _mixer_kernel(x_ref, nw_ref, win_ref, gb_ref, pw_ref, ps_ref, pup_ref,
                  lbf_ref, oml_ref, dl_ref, hnw_ref, hup_ref, wout_ref,
                  o_ref, carry_ref, state_ref):
    tm = x_ref.shape[1]
    d = x_ref.shape[2]
    pw_cols = pup_ref.shape[0]
    hw = hup_ref.shape[0]
    gd = pw_cols // N_POOL_GROUPS
    hd = hw // HGRN_HEADS
    off_q = pw_cols
    off_f = off_q + hw
    off_i = off_f + hw
    off_g = off_i + hw
    off_gates = off_g + hw
    si = pl.program_id(1)

    @pl.when(si == 0)
    def _():
        carry_ref[...] = jnp.zeros_like(carry_ref)
        state_ref[...] = jnp.zeros_like(state_ref)

    x = x_ref[0]
    h = _rmsnorm(x, nw_ref[...]).astype(BF16)

    def pool_branch():
        u = _dot(h, win_ref[:, 0:pw_cols])
        ext = jnp.concatenate([carry_ref[...], u], axis=0)
        carry_ref[...] = u[tm - POOL_CARRY:, :]
        pos = (si * tm + 1 + lax.broadcasted_iota(jnp.int32, (tm, 1), 0)).astype(F32)
        acc = ext
        span = 1
        mixed = []
        for gi, w in enumerate(POOL_WINDOWS):
            while span < w:
                acc = acc + pltpu.roll(acc, span, 0)
                span *= 2
            wsum = acc[POOL_CARRY:, 0:gd]
            acc = acc[:, gd:] if gi + 1 < N_POOL_GROUPS else None
            ug = u[:, gi * gd:(gi + 1) * gd]
            pooled = wsum * (1.0 / jnp.minimum(pos, float(w))) - ug
            mixed.append(_dot(pooled.astype(BF16), pw_ref[gi]))
        mixed = jnp.concatenate(mixed, axis=1) * ps_ref[...]
        return _dot(mixed.astype(BF16), pup_ref[...])

    q_pre = _dot(h, win_ref[:, off_q:off_f])
    q = q_pre * _sigmoid(q_pre)
    fp = _dot(h, win_ref[:, off_f:off_i])
    v = _dot(h, win_ref[:, off_i:off_g])
    g_pre = _dot(h, win_ref[:, off_g:off_gates])
    g = g_pre * _sigmoid(g_pre)

    en = jnp.exp(-jnp.abs(fp))
    rden = 1.0 / (1.0 + en)
    nonneg = fp >= 0
    spos = jnp.where(nonneg, 1.0, en) * rden
    sneg = jnp.where(nonneg, en, 1.0) * rden
    c = jnp.log2(lbf_ref[...] + oml_ref[...] * spos)
    k = oml_ref[...] * sneg + dl_ref[...]

    gz_parts = []
    y_pool_out = []
    gate_cols = 2 * LANE
    side_work = [
        functools.partial(
            lambda c0: gz_parts.append(_dot(h, win_ref[:, off_gates + c0:off_gates + c0 + gate_cols])),
            c0)
        for c0 in range(0, 2 * d, gate_cols)]
    side_work.insert(len(side_work) // 2, lambda: y_pool_out.append(pool_branch()))
    ck = HGRN_CHUNK
    o_chunks = [_hgrn_chunk(q[r0:r0 + ck], k[r0:r0 + ck], v[r0:r0 + ck], c[r0:r0 + ck],
                            state_ref, hd, side_work)
                for r0 in range(0, tm, ck)]
    while side_work:
        side_work.pop(0)()
    o_all = jnp.concatenate(o_chunks, axis=0) * hnw_ref[...] * g
    y_hgrn = _dot(o_all.astype(BF16), hup_ref[...])

    gz = jnp.concatenate(gz_parts, axis=1) + gb_ref[...]
    gates = _sigmoid(gz)
    merged = gates[:, 0:d] * y_pool_out[0] + gates[:, d:2 * d] * y_hgrn
    o_ref[0] = x + _dot(merged.astype(BF16), wout_ref[...])


def _token_mixer(x, nw, win, gb, pw, ps, pup, lbf, oml, dl, hnw, hup, wout):
    b, s, d = x.shape
    tm = MIXER_TM
    pw_cols = pup.shape[0]
    hw = hup.shape[0]
    hd = hw // HGRN_HEADS
    assert s % tm == 0 and tm >= POOL_CARRY
    consts = (nw, win, gb, pw, ps, pup, lbf, oml, dl, hnw, hup, wout)
    return pl.pallas_call(
        _mixer_kernel,
        grid=(b, s // tm),
        in_specs=[pl.BlockSpec((1, tm, d), lambda bi, si: (bi, si, 0))]
        + [_const_spec(a.shape) for a in consts],
        out_specs=pl.BlockSpec((1, tm, d), lambda bi, si: (bi, si, 0)),
        out_shape=jax.ShapeDtypeStruct(x.shape, F32),
        scratch_shapes=[pltpu.VMEM((POOL_CARRY, pw_cols), F32),
                        pltpu.VMEM((HGRN_HEADS, hd, hd), F32)],
        compiler_params=pltpu.CompilerParams(
            dimension_semantics=("arbitrary", "arbitrary"),
            vmem_limit_bytes=VMEM_LIMIT),
        name="token_mixer",
    )(x, *consts)


def _dense_ffn_kernel(x_ref, nw_ref, wg_ref, wu_ref, wd_ref, o_ref, *, n_chunks):
    x = x_ref[...]
    h = _rmsnorm(x, nw_ref[...]).astype(BF16)
    ff = wg_ref.shape[1]
    fc = ff // n_chunks
    out = x
    for ci in range(n_chunks):
        sl = slice(ci * fc, (ci + 1) * fc)
        gt = _dot(h, wg_ref[:, sl])
        up = _dot(h, wu_ref[:, sl])
        a = (gt * _sigmoid(gt) * up).astype(BF16)
        out = out + _dot(a, wd_ref[sl, :])
    o_ref[...] = out


def _dense_ffn(x2, nw, wg, wu, wd):
    t, d = x2.shape
    tm = DENSE_TM
    ff = wg.shape[1]
    n_chunks = 2 if ff % (2 * LANE) == 0 else 1
    assert t % tm == 0
    return pl.pallas_call(
        functools.partial(_dense_ffn_kernel, n_chunks=n_chunks),
        grid=(t // tm,),
        in_specs=[pl.BlockSpec((tm, d), lambda i: (i, 0)),
                  _const_spec(nw.shape), _const_spec(wg.shape),
                  _const_spec(wu.shape), _const_spec(wd.shape)],
        out_specs=pl.BlockSpec((tm, d), lambda i: (i, 0)),
        out_shape=jax.ShapeDtypeStruct(x2.shape, F32),
        compiler_params=pltpu.CompilerParams(
            dimension_semantics=("arbitrary",), vmem_limit_bytes=VMEM_LIMIT),
        name="dense_ffn",
    )(x2, nw, wg, wu, wd)


def _router_kernel(x_ref, nw_ref, r_ref, h_ref, idx_ref, w_ref):
    x = x_ref[...]
    h = _rmsnorm(x, nw_ref[...])
    h_ref[...] = h.astype(BF16)
    ne = r_ref.shape[0]
    logit = [jnp.sum(h * r_ref[e:e + 1, :], axis=-1, keepdims=True) for e in range(ne)]
    m1 = logit[0]
    i1 = jnp.zeros(m1.shape, jnp.int32)
    for e in range(1, ne):
        better = logit[e] > m1
        m1 = jnp.where(better, logit[e], m1)
        i1 = jnp.where(better, e, i1)
    m2 = jnp.full(m1.shape, -jnp.inf, F32)
    i2 = jnp.zeros(m1.shape, jnp.int32)
    for e in range(ne):
        cand = jnp.where(i1 == e, -jnp.inf, logit[e])
        better = cand > m2
        m2 = jnp.where(better, cand, m2)
        i2 = jnp.where(better, e, i2)
    e2 = jnp.exp(m2 - m1)
    den = 1.0 + e2
    slot = lax.broadcasted_iota(jnp.int32, (x.shape[0], TOP_K), 1)
    idx_ref[...] = jnp.where(slot == 0, i1, i2)
    w_ref[...] = jnp.where(slot == 0, 1.0 / den, e2 / den)


def _router(x2, nw, router):
    t, d = x2.shape
    tm = ROUTER_TM
    assert t % tm == 0
    return pl.pallas_call(
        _router_kernel,
        grid=(t // tm,),
        in_specs=[pl.BlockSpec((tm, d), lambda i: (i, 0)),
                  _const_spec(nw.shape), _const_spec(router.shape)],
        out_specs=[pl.BlockSpec((tm, d), lambda i: (i, 0)),
                   pl.BlockSpec((tm, TOP_K), lambda i: (i, 0)),
                   pl.BlockSpec((tm, TOP_K), lambda i: (i, 0))],
        out_shape=[jax.ShapeDtypeStruct((t, d), BF16),
                   jax.ShapeDtypeStruct((t, TOP_K), jnp.int32),
                   jax.ShapeDtypeStruct((t, TOP_K), F32)],
        compiler_params=pltpu.CompilerParams(
            dimension_semantics=("arbitrary",), vmem_limit_bytes=VMEM_LIMIT),
        name="moe_router",
    )(x2, nw, router)


def _route(top_i, n_experts, bm, tc):
    t = top_i.shape[0]
    ex = jnp.arange(n_experts, dtype=jnp.int32)
    sel = top_i[:, :, None] == ex[None, None, :]
    chosen = jnp.any(sel, axis=1).astype(jnp.int32)
    csum = jnp.cumsum(chosen, axis=0)
    counts = csum[-1]
    nb_e = (counts + bm - 1) // bm + 1
    blk_end = jnp.cumsum(nb_e)
    blk_start = blk_end - nb_e
    row0 = blk_start * bm
    seli = sel.astype(jnp.int32)
    pos = jnp.sum(seli * (csum - 1 + row0[None, :])[:, None, :], axis=2)

    nb = (t * TOP_K) // bm + 2 * n_experts
    bi = jnp.arange(nb, dtype=jnp.int32)
    blk_e = jnp.minimum(jnp.sum((bi[:, None] >= blk_end[None, :]).astype(jnp.int32), axis=1),
                        n_experts - 1)
    blk_valid = jnp.clip(counts[blk_e] - (bi - blk_start[blk_e]) * bm, 0, bm)
    blk_valid = jnp.where(bi < blk_end[-1], blk_valid, 0)

    tile_end = csum[tc - 1::tc]
    tile_prev = jnp.concatenate([jnp.zeros((1, n_experts), jnp.int32), tile_end[:-1]], axis=0)
    pair_row = row0[None, :] + tile_prev
    pair_cnt = tile_end - tile_prev
    i32 = lambda a: a.astype(jnp.int32)
    return i32(pos), i32(blk_e), i32(blk_valid), i32(pair_row), i32(pair_cnt)


def _dispatch_window_write(wbuf, x_hbm, sem, start, slot, e):
    return pltpu.make_async_copy(wbuf.at[slot, e], x_hbm.at[pl.ds(start, DISPATCH_WIN), :],
                                 sem.at[slot, e])


def _dispatch_kernel(row_ref, cnt_ref, h_ref, idx_ref, pos_ref, zeros_hbm, x_hbm,
                     wbuf, xbuf, carry, sem, xsem):
    del zeros_hbm
    j = pl.program_id(0)
    nt = pl.num_programs(0)
    ne = carry.shape[0]
    win = DISPATCH_WIN
    grp = BF16_SUBLANES
    slot = j % 2

    @pl.when(j == 0)
    def _():
        carry[...] = jnp.zeros_like(carry)

    @pl.when(j > 0)
    def _():
        for e in range(ne):
            _dispatch_window_write(wbuf, x_hbm, sem, 0, 1 - slot, e).wait()

    h = h_ref[...]
    idx = idx_ref[0]
    pos = pos_ref[0]
    rows = lax.broadcasted_iota(jnp.int32, (win, 1), 0)

    def selection(e, start):
        hit = None
        for kk in range(TOP_K):
            off = jnp.where(idx[kk:kk + 1, :] == e, pos[kk:kk + 1, :] - start, -1)
            hk = off == rows
            hit = hk if hit is None else jnp.logical_or(hit, hk)
        return jnp.where(hit, 1.0, 0.0).astype(BF16)

    for e in range(ne):
        row = row_ref[j * ne + e]
        cnt = cnt_ref[j * ne + e]
        a0 = pl.multiple_of((row // grp) * grp, grp)
        fill = row - a0 + cnt
        w = _dot(selection(e, a0), h)
        head = w[0:grp, :] + carry[e].astype(F32)
        wbuf[slot, e, 0:grp, :] = head.astype(BF16)
        wbuf[slot, e, grp:, :] = w[grp:, :].astype(BF16)
        _dispatch_window_write(wbuf, x_hbm, sem, a0, slot, e).start()
        n_more = fill // win

        @pl.when(n_more == 0)
        def _(e=e, fill=fill):
            g = pl.multiple_of((fill // grp) * grp, grp)
            carry[e] = wbuf[slot, e, pl.ds(g, grp), :]

        @pl.when(n_more > 0)
        def _(e=e, a0=a0, fill=fill, n_more=n_more):
            def more(m, c):
                st = pl.multiple_of(a0 + (m + 1) * win, grp)
                xbuf[...] = _dot(selection(e, st), h).astype(BF16)
                cp = pltpu.make_async_copy(xbuf, x_hbm.at[pl.ds(st, win), :], xsem.at[0])
                cp.start()
                cp.wait()
                return c

            lax.fori_loop(0, n_more, more, 0)
            g = pl.multiple_of(((fill - n_more * win) // grp) * grp, grp)
            carry[e] = xbuf[pl.ds(g, grp), :]

    @pl.when(j == nt - 1)
    def _():
        for e in range(ne):
            _dispatch_window_write(wbuf, x_hbm, sem, 0, slot, e).wait()


def _moe_dispatch(h, top_i, pos, pair_row, pair_cnt, n_rows, tc):
    t, d = h.shape
    nt = t // tc
    ne = pair_row.shape[1]
    lane_major = lambda a: a.reshape(nt, tc, TOP_K).transpose(0, 2, 1)
    grid_spec = pltpu.PrefetchScalarGridSpec(
        num_scalar_prefetch=2,
        grid=(nt,),
        in_specs=[pl.BlockSpec((tc, d), lambda j, a, b: (j, 0)),
                  pl.BlockSpec((1, TOP_K, tc), lambda j, a, b: (j, 0, 0)),
                  pl.BlockSpec((1, TOP_K, tc), lambda j, a, b: (j, 0, 0)),
                  pl.BlockSpec(memory_space=pl.ANY)],
        out_specs=pl.BlockSpec(memory_space=pl.ANY),
        scratch_shapes=[pltpu.VMEM((2, ne, DISPATCH_WIN, d), BF16),
                        pltpu.VMEM((DISPATCH_WIN, d), BF16),
                        pltpu.VMEM((ne, BF16_SUBLANES, d), BF16),
                        pltpu.SemaphoreType.DMA((2, ne)),
                        pltpu.SemaphoreType.DMA((1,))],
    )
    return pl.pallas_call(
        _dispatch_kernel,
        grid_spec=grid_spec,
        out_shape=jax.ShapeDtypeStruct((n_rows, d), BF16),
        input_output_aliases={5: 0},
        compiler_params=pltpu.CompilerParams(
            dimension_semantics=("arbitrary",), vmem_limit_bytes=VMEM_LIMIT),
        name="moe_dispatch",
    )(pair_row.reshape(-1), pair_cnt.reshape(-1), h, lane_major(top_i), lane_major(pos),
      jnp.zeros((n_rows, d), BF16))


def _experts_kernel(blk_e_ref, valid_ref, x_ref, wg_ref, wu_ref, wd_ref, o_ref, acc):
    del blk_e_ref
    i = pl.program_id(0)
    f = pl.program_id(1)
    nf = pl.num_programs(1)
    active = valid_ref[i] > 0

    @pl.when(active)
    def _():
        xb = x_ref[...]
        gt = _dot(xb, wg_ref[0])
        up = _dot(xb, wu_ref[0])
        a = (gt * _sigmoid(gt) * up).astype(BF16)
        part = _dot(a, wd_ref[0])

        @pl.when(f == 0)
        def _():
            acc[...] = part

        @pl.when(f > 0)
        def _():
            acc[...] = acc[...] + part

    @pl.when(f == nf - 1)
    def _():
        @pl.when(active)
        def _():
            o_ref[...] = acc[...].astype(o_ref.dtype)

        @pl.when(jnp.logical_not(active))
        def _():
            o_ref[...] = jnp.zeros_like(o_ref)


def _moe_experts(x_sorted, blk_e, blk_valid, wg, wu, wd, bm):
    rows, d = x_sorted.shape
    nb = rows // bm
    tf = MOE_TF
    ff = wg.shape[2]
    assert ff % tf == 0
    nf = ff // tf

    def wcol(i, f, bv):
        return jnp.where(bv[i] > 0, f, nf - 1)

    grid_spec = pltpu.PrefetchScalarGridSpec(
        num_scalar_prefetch=2,
        grid=(nb, nf),
        in_specs=[
            pl.BlockSpec((bm, d), lambda i, f, be, bv: (i, 0)),
            pl.BlockSpec((1, d, tf), lambda i, f, be, bv: (be[i], 0, wcol(i, f, bv))),
            pl.BlockSpec((1, d, tf), lambda i, f, be, bv: (be[i], 0, wcol(i, f, bv))),
            pl.BlockSpec((1, tf, d), lambda i, f, be, bv: (be[i], wcol(i, f, bv), 0)),
        ],
        out_specs=pl.BlockSpec((bm, d), lambda i, f, be, bv: (i, 0)),
        scratch_shapes=[pltpu.VMEM((bm, d), F32)],
    )
    return pl.pallas_call(
        _experts_kernel,
        grid_spec=grid_spec,
        out_shape=jax.ShapeDtypeStruct((rows, d), BF16),
        compiler_params=pltpu.CompilerParams(
            dimension_semantics=("arbitrary", "arbitrary"),
            vmem_limit_bytes=VMEM_LIMIT),
        name="moe_experts",
    )(blk_e, blk_valid, x_sorted, wg, wu, wd)


def _window_start(row, n_rows):
    a0 = (row // BF16_SUBLANES) * BF16_SUBLANES
    return pl.multiple_of(jnp.minimum(a0, n_rows - COMBINE_WIN), BF16_SUBLANES)


def _window_copy(y_hbm, ybuf, sem, start, slot, e):
    return pltpu.make_async_copy(y_hbm.at[pl.ds(start, COMBINE_WIN), :],
                                 ybuf.at[slot, pl.ds(e * COMBINE_WIN, COMBINE_WIN), :],
                                 sem.at[slot, e])


def _combine_kernel(row_ref, cnt_ref, x_ref, idx_ref, w_ref, pos_ref, nw_ref, y_hbm,
                    o_ref, ybuf, mbuf, xtra, zacc, sem, xsem, *, apply_norm):
    j = pl.program_id(0)
    nt = pl.num_programs(0)
    ne = mbuf.shape[1] // COMBINE_WIN
    n_rows = y_hbm.shape[0]
    slot = j % 2

    def start_tile(tile, s):
        for e in range(ne):
            _window_copy(y_hbm, ybuf, sem, _window_start(row_ref[tile * ne + e], n_rows), s, e).start()

    @pl.when(j == 0)
    def _():
        start_tile(0, 0)

    @pl.when(j + 1 < nt)
    def _():
        start_tile(j + 1, 1 - slot)

    idx = idx_ref[...]
    wts = w_ref[...]
    pos = pos_ref[...]
    lanes = lax.broadcasted_iota(jnp.int32, (1, COMBINE_WIN), 1)

    def weights_tile(e, start, skip):
        m = jnp.zeros((idx.shape[0], COMBINE_WIN), F32)
        for kk in range(TOP_K):
            off = pos[:, kk:kk + 1] - start
            valid = jnp.logical_and(idx[:, kk:kk + 1] == e, off >= skip)
            off = jnp.where(valid, off, -1)
            m = jnp.where(off == lanes, wts[:, kk:kk + 1], m)
        return m.astype(BF16)

    def finish(s):
        starts = [_window_start(row_ref[j * ne + e], n_rows) for e in range(ne)]
        for e in range(ne):
            mbuf[:, e * COMBINE_WIN:(e + 1) * COMBINE_WIN] = weights_tile(e, starts[e], 0)
        for e in range(ne):
            _window_copy(y_hbm, ybuf, sem, starts[e], s, e).wait()
        z = x_ref[...] + _dot(mbuf[...], ybuf[s])
        n_more = [jnp.maximum((row_ref[j * ne + e] - starts[e] + cnt_ref[j * ne + e]
                               + COMBINE_WIN - 1) // COMBINE_WIN - 1, 0) for e in range(ne)]
        any_more = functools.reduce(lambda a, b: a + b, n_more)

        def emit(zz):
            o_ref[...] = _rmsnorm(zz, nw_ref[...]) if apply_norm else zz

        @pl.when(any_more == 0)
        def _():
            emit(z)

        @pl.when(any_more > 0)
        def _():
            zacc[...] = z
            for e in range(ne):
                def more(m, carry, e=e, start=starts[e]):
                    want = start + (m + 1) * COMBINE_WIN
                    st = pl.multiple_of(jnp.minimum(want, n_rows - COMBINE_WIN), BF16_SUBLANES)
                    cp = pltpu.make_async_copy(y_hbm.at[pl.ds(st, COMBINE_WIN), :], xtra, xsem.at[0])
                    cp.start()
                    cp.wait()
                    zacc[...] += _dot(weights_tile(e, st, want - st), xtra[...])
                    return carry

                lax.fori_loop(0, n_more[e], more, 0)
            emit(zacc[...])

    finish(slot)


def _moe_combine(x2, y_sorted, top_i, top_w, pos, pair_row, pair_cnt, nw, apply_norm):
    t, d = x2.shape
    tm = MOE_TC
    nt = t // tm
    ne = pair_row.shape[1]
    assert y_sorted.shape[0] >= COMBINE_WIN
    grid_spec = pltpu.PrefetchScalarGridSpec(
        num_scalar_prefetch=2,
        grid=(nt,),
        in_specs=[
            pl.BlockSpec((tm, d), lambda i, a, b: (i, 0)),
            pl.BlockSpec((tm, TOP_K), lambda i, a, b: (i, 0)),
            pl.BlockSpec((tm, TOP_K), lambda i, a, b: (i, 0)),
            pl.BlockSpec((tm, TOP_K), lambda i, a, b: (i, 0)),
            pl.BlockSpec(nw.shape, lambda i, a, b: (0, 0)),
            pl.BlockSpec(memory_space=pl.ANY),
        ],
        out_specs=pl.BlockSpec((tm, d), lambda i, a, b: (i, 0)),
        scratch_shapes=[pltpu.VMEM((2, ne * COMBINE_WIN, d), BF16),
                        pltpu.VMEM((tm, ne * COMBINE_WIN), BF16),
                        pltpu.VMEM((COMBINE_WIN, d), BF16),
                        pltpu.VMEM((tm, d), F32),
                        pltpu.SemaphoreType.DMA((2, ne)),
                        pltpu.SemaphoreType.DMA((1,))],
    )
    return pl.pallas_call(
        functools.partial(_combine_kernel, apply_norm=apply_norm),
        grid_spec=grid_spec,
        out_shape=jax.ShapeDtypeStruct((t, d), F32),
        compiler_params=pltpu.CompilerParams(
            dimension_semantics=("arbitrary",), vmem_limit_bytes=VMEM_LIMIT),
        name="moe_combine",
    )(pair_row.reshape(-1), pair_cnt.reshape(-1), x2, top_i, top_w, pos, nw, y_sorted)


def _final_norm_kernel(x_ref, nw_ref, o_ref):
    o_ref[...] = _rmsnorm(x_ref[...], nw_ref[...])


def _final_norm(x2, nw):
    t, d = x2.shape
    tm = DENSE_TM
    return pl.pallas_call(
        _final_norm_kernel,
        grid=(t // tm,),
        in_specs=[pl.BlockSpec((tm, d), lambda i: (i, 0)), _const_spec(nw.shape)],
        out_specs=pl.BlockSpec((tm, d), lambda i: (i, 0)),
        out_shape=jax.ShapeDtypeStruct((t, d), F32),
        name="final_norm",
    )(x2, nw)


def kernel(x, attn_norm_w, w_in, gate_bias, pool_w, pool_scale, pool_up, hgrn_lower_bounds,
           hgrn_norm_w, hgrn_up, w_out, ffn_norm_w, dense_w_gate, dense_w_up, dense_w_down,
           moe_router, moe_w_gate, moe_w_up, moe_w_down, final_norm_w):
    b, s, d = x.shape
    depth = w_in.shape[0]
    n_experts = moe_router.shape[-1]

    lb_soft = jax.nn.softmax(hgrn_lower_bounds.astype(F32), axis=0)
    lb = jnp.clip(jnp.cumsum(lb_soft, axis=0) - lb_soft[0:1], 0.0, 1.0 - 1e-6)
    lbf = jnp.maximum(lb, LB_FLOOR)
    oml = 1.0 - lb
    dlb = lb - lbf

    def row(a):
        return a.reshape(1, -1).astype(F32)

    final_w = row(final_norm_w)
    normed = None
    for l in range(depth):
        x = _token_mixer(
            x, row(attn_norm_w[l]), w_in[l].astype(BF16), row(gate_bias[l]),
            pool_w[l].astype(BF16), row(pool_scale[l]), pool_up[l].astype(BF16),
            row(lbf[l]), row(oml[l]), row(dlb[l]),
            row(jnp.tile(hgrn_norm_w[l], HGRN_HEADS)), hgrn_up[l].astype(BF16),
            w_out[l].astype(BF16))
        x2 = x.reshape(b * s, d)
        j = l // 2
        last = l == depth - 1
        if l % 2 == 0:
            x2 = _dense_ffn(x2, row(ffn_norm_w[l]), dense_w_gate[j].astype(BF16),
                            dense_w_up[j].astype(BF16), dense_w_down[j].astype(BF16))
            if last:
                normed = _final_norm(x2, final_w)
        else:
            h, top_i, top_w = _router(x2, row(ffn_norm_w[l]), moe_router[j].astype(F32).T)
            pos, blk_e, blk_valid, pair_row, pair_cnt = _route(top_i, n_experts, MOE_BM, MOE_TC)
            x_sorted = _moe_dispatch(h, top_i, pos, pair_row, pair_cnt,
                                     blk_e.shape[0] * MOE_BM, MOE_TC)
            y_sorted = _moe_experts(x_sorted, blk_e, blk_valid, moe_w_gate[j].astype(BF16),
                                    moe_w_up[j].astype(BF16), moe_w_down[j].astype(BF16), MOE_BM)
            x2 = _moe_combine(x2, y_sorted, top_i, top_w, pos, pair_row, pair_cnt, final_w,
                              apply_norm=last)
            if last:
                normed = x2
        x = x2.reshape(b, s, d)
    return normed.reshape(b, s, d)
```

```python
import functools

import jax
import jax.numpy as jnp
from jax import lax
from jax.experimental import pallas as pl
from jax.experimental.pallas import tpu as pltpu

F32 = jnp.float32
BF16 = jnp.bfloat16

EPS = 1e-6
LB_FLOOR = 1e-30
POOL_WINDOWS = (2, 4, 8, 16)
N_POOL_GROUPS = 4
HGRN_HEADS = 4
TOP_K = 2
LANE = 128
BF16_SUBLANES = 16

MIXER_TM = 512
HGRN_CHUNK = 256
POOL_CARRY = 16
DENSE_TM = 512
ROUTER_TM = 512
MOE_BM = 512
MOE_TF = 1792
MOE_TC = 512
DISPATCH_WIN = 192
COMBINE_WIN = 256
VMEM_LIMIT = 56 * 1024 * 1024


def _const_spec(shape):
    nd = len(shape)
    return pl.BlockSpec(shape, lambda *_: (0,) * nd, pipeline_mode=pl.Buffered(1))


def _dot(a, b):
    return jnp.dot(a, b, preferred_element_type=F32)


def _dot_nt(a, b):
    return lax.dot_general(a, b, (((1,), (1,)), ((), ())), preferred_element_type=F32)


def _dot_tn(a, b):
    return lax.dot_general(a, b, (((0,), (0,)), ((), ())), preferred_element_type=F32)


def _sigmoid(x):
    return 0.5 * jnp.tanh(0.5 * x) + 0.5


def _rmsnorm(x, w):
    ms = jnp.mean(x * x, axis=-1, keepdims=True)
    return x * lax.rsqrt(ms + EPS) * w


def _hgrn_chunk(q, k, v, c, state_ref, hd, side_work):
    n = q.shape[0]
    rows = lax.broadcasted_iota(jnp.int32, (n, 1), 0)
    cols = lax.broadcasted_iota(jnp.int32, (1, n), 1)
    v16 = v.astype(BF16)
    scores = [jnp.zeros((n, n), F32) for _ in range(HGRN_HEADS)]

    p = c
    tot = c
    half = 1
    shift = 1
    while half < n:
        upper = (rows & half) != 0
        e = jnp.exp2(jnp.where(upper, p, tot - p))
        row_key = jnp.where(upper, rows >> shift, -1)
        col_key = jnp.where((cols & half) != 0, -2, cols >> shift)
        owned = row_key == col_key
        for hi in range(HGRN_HEADS):
            sl = slice(hi * hd, (hi + 1) * hd)
            eh = e[:, sl]
            qt = (q[:, sl] * eh).astype(BF16)
            kt = (k[:, sl] * eh).astype(BF16)
            scores[hi] = jnp.where(owned, _dot_nt(qt, kt), scores[hi])
        other = jnp.where(upper, pltpu.roll(tot, half, 0), pltpu.roll(tot, n - half, 0))
        p = p + jnp.where(upper, other, 0.0)
        tot = tot + other
        if side_work:
            side_work.pop(0)()
        half *= 2
        shift += 1

    eg = jnp.exp2(p)
    er = jnp.exp2(tot - p)
    o_heads = []
    for hi in range(HGRN_HEADS):
        sl = slice(hi * hd, (hi + 1) * hd)
        qh, kh, vh = q[:, sl], k[:, sl], v[:, sl]
        st = state_ref[hi]
        o = _dot(scores[hi].astype(BF16), v16[:, sl])
        o = o + jnp.sum(qh * kh, axis=-1, keepdims=True) * vh
        o = o + _dot_nt((qh * eg[:, sl]).astype(BF16), st.astype(BF16))
        kd = (kh * er[:, sl]).astype(BF16)
        state_ref[hi] = st * jnp.exp2(tot[0:1, sl]) + _dot_tn(v16[:, sl], kd)
        o = o * lax.rsqrt(jnp.mean(o * o, axis=-1, keepdims=True) + EPS)
        o_heads.append(o)
    return jnp.concatenate(o_heads, axis=1)


def _mixer_kernel(x_ref, nw_ref, win_ref, gb_ref, pw_ref, ps_ref, pup_ref,
                  lbf_ref, oml_ref, dl_ref, hnw_ref, hup_ref, wout_ref,
                  o_ref, carry_ref, state_ref):
    tm = x_ref.shape[1]
    d = x_ref.shape[2]
    pw_cols = pup_ref.shape[0]
    hw = hup_ref.shape[0]
    gd = pw_cols // N_POOL_GROUPS
    hd = hw // HGRN_HEADS
    off_q = pw_cols
    off_f = off_q + hw
    off_i = off_f + hw
    off_g = off_i + hw
    off_gates = off_g + hw
    si = pl.program_id(1)

    @pl.when(si == 0)
    def _():
        carry_ref[...] = jnp.zeros_like(carry_ref)
        state_ref[...] = jnp.zeros_like(state_ref)

    x = x_ref[0]
    h = _rmsnorm(x, nw_ref[...]).astype(BF16)

    def pool_branch():
        u = _dot(h, win_ref[:, 0:pw_cols])
        ext = jnp.concatenate([carry_ref[...], u], axis=0)
        carry_ref[...] = u[tm - POOL_CARRY:, :]
        pos = (si * tm + 1 + lax.broadcasted_iota(jnp.int32, (tm, 1), 0)).astype(F32)
        acc = ext
        span = 1
        mixed = []
        for gi, w in enumerate(POOL_WINDOWS):
            while span < w:
                acc = acc + pltpu.roll(acc, span, 0)
                span *= 2
            wsum = acc[POOL_CARRY:, 0:gd]
            acc = acc[:, gd:] if gi + 1 < N_POOL_GROUPS else None
            ug = u[:, gi * gd:(gi + 1) * gd]
            pooled = wsum * (1.0 / jnp.minimum(pos, float(w))) - ug
            mixed.append(_dot(pooled.astype(BF16), pw_ref[gi]))
        mixed = jnp.concatenate(mixed, axis=1) * ps_ref[...]
        return _dot(mixed.astype(BF16), pup_ref[...])

    q_pre = _dot(h, win_ref[:, off_q:off_f])
    q = q_pre * _sigmoid(q_pre)
    fp = _dot(h, win_ref[:, off_f:off_i])
    v = _dot(h, win_ref[:, off_i:off_g])
    g_pre = _dot(h, win_ref[:, off_g:off_gates])
    g = g_pre * _sigmoid(g_pre)

    en = jnp.exp(-jnp.abs(fp))
    rden = 1.0 / (1.0 + en)
    nonneg = fp >= 0
    spos = jnp.where(nonneg, 1.0, en) * rden
    sneg = jnp.where(nonneg, en, 1.0) * rden
    c = jnp.log2(lbf_ref[...] + oml_ref[...] * spos)
    k = oml_ref[...] * sneg + dl_ref[...]

    gz_parts = []
    y_pool_out = []
    gate_cols = 2 * LANE
    side_work = [
        functools.partial(
            lambda c0: gz_parts.append(_dot(h, win_ref[:, off_gates + c0:off_gates + c0 + gate_cols])),
            c0)
        for c0 in range(0, 2 * d, gate_cols)]
    side_work.insert(len(side_work) // 2, lambda: y_pool_out.append(pool_branch()))
    ck = HGRN_CHUNK
    o_chunks = [_hgrn_chunk(q[r0:r0 + ck], k[r0:r0 + ck], v[r0:r0 + ck], c[r0:r0 + ck],
                            state_ref, hd, side_work)
                for r0 in range(0, tm, ck)]
    while side_work:
        side_work.pop(0)()
    o_all = jnp.concatenate(o_chunks, axis=0) * hnw_ref[...] * g
    y_hgrn = _dot(o_all.astype(BF16), hup_ref[...])

    gz = jnp.concatenate(gz_parts, axis=1) + gb_ref[...]
    gates = _sigmoid(gz)
    merged = gates[:, 0:d] * y_pool_out[0] + gates[:, d:2 * d] * y_hgrn
    o_ref[0] = x + _dot(merged.astype(BF16), wout_ref[...])


def _token_mixer(x, nw, win, gb, pw, ps, pup, lbf, oml, dl, hnw, hup, wout):
    b, s, d = x.shape
    tm = MIXER_TM
    pw_cols = pup.shape[0]
    hw = hup.shape[0]
    hd = hw // HGRN_HEADS
    assert s % tm == 0 and tm >= POOL_CARRY
    consts = (nw, win, gb, pw, ps, pup, lbf, oml, dl, hnw, hup, wout)
    return pl.pallas_call(
        _mixer_kernel,
        grid=(b, s // tm),
        in_specs=[pl.BlockSpec((1, tm, d), lambda bi, si: (bi, si, 0))]
        + [_const_spec(a.shape) for a in consts],
        out_specs=pl.BlockSpec((1, tm, d), lambda bi, si: (bi, si, 0)),
        out_shape=jax.ShapeDtypeStruct(x.shape, F32),
        scratch_shapes=[pltpu.VMEM((POOL_CARRY, pw_cols), F32),
                        pltpu.VMEM((HGRN_HEADS, hd, hd), F32)],
        compiler_params=pltpu.CompilerParams(
            dimension_semantics=("arbitrary", "arbitrary"),
            vmem_limit_bytes=VMEM_LIMIT),
        name="token_mixer",
    )(x, *consts)


def _dense_ffn_kernel(x_ref, nw_ref, wg_ref, wu_ref, wd_ref, *rest, n_chunks, n_cast):
    cast_in, o_ref = rest[:n_cast], rest[n_cast]
    cast_out, zero_out = rest[n_cast + 1:2 * n_cast + 1], rest[2 * n_cast + 1:]
    x = x_ref[...]
    h = _rmsnorm(x, nw_ref[...]).astype(BF16)
    ff = wg_ref.shape[1]
    fc = ff // n_chunks
    out = x
    for ci in range(n_chunks):
        sl = slice(ci * fc, (ci + 1) * fc)
        gt = _dot(h, wg_ref[:, sl])
        up = _dot(h, wu_ref[:, sl])
        a = (gt * _sigmoid(gt) * up).astype(BF16)
        out = out + _dot(a, wd_ref[sl, :])
    o_ref[...] = out
    for src, dst in zip(cast_in, cast_out):
        dst[...] = src[...].astype(dst.dtype)
    for z in zero_out:
        z[...] = jnp.zeros_like(z)


def _dense_ffn(x2, nw, wg, wu, wd, to_cast=(), zero_rows=0):
    t, d = x2.shape
    tm = DENSE_TM
    ff = wg.shape[1]
    n_chunks = 2 if ff % (2 * LANE) == 0 else 1
    assert t % tm == 0
    n = t // tm
    slabs = []
    for a in to_cast:
        rows = a.size // a.shape[-1]
        assert rows % (n * BF16_SUBLANES) == 0
        slabs.append(a.reshape(n, rows // n, a.shape[-1]))
    slab_specs = [pl.BlockSpec((1,) + s.shape[1:], lambda i: (i, 0, 0)) for s in slabs]
    zero_specs, zero_shapes = [], []
    if zero_rows:
        assert zero_rows % (n * BF16_SUBLANES) == 0
        zero_specs = [pl.BlockSpec((zero_rows // n, d), lambda i: (i, 0))]
        zero_shapes = [jax.ShapeDtypeStruct((zero_rows, d), BF16)]
    outs = pl.pallas_call(
        functools.partial(_dense_ffn_kernel, n_chunks=n_chunks, n_cast=len(slabs)),
        grid=(n,),
        in_specs=[pl.BlockSpec((tm, d), lambda i: (i, 0)),
                  _const_spec(nw.shape), _const_spec(wg.shape),
                  _const_spec(wu.shape), _const_spec(wd.shape)] + slab_specs,
        out_specs=[pl.BlockSpec((tm, d), lambda i: (i, 0))] + slab_specs + zero_specs,
        out_shape=[jax.ShapeDtypeStruct(x2.shape, F32)]
        + [jax.ShapeDtypeStruct(s.shape, BF16) for s in slabs] + zero_shapes,
        compiler_params=pltpu.CompilerParams(
            dimension_semantics=("arbitrary",), vmem_limit_bytes=VMEM_LIMIT),
        name="dense_ffn",
    )(x2, nw, wg, wu, wd, *slabs)
    casted = [o.reshape(a.shape) for o, a in zip(outs[1:1 + len(slabs)], to_cast)]
    return outs[0], casted, (outs[-1] if zero_rows else None)


def _router_kernel(x_ref, nw_ref, r_ref, h_ref, idx_ref, w_ref):
    x = x_ref[...]
    h = _rmsnorm(x, nw_ref[...])
    h_ref[...] = h.astype(BF16)
    ne = r_ref.shape[0]
    logit = [jnp.sum(h * r_ref[e:e + 1, :], axis=-1, keepdims=True) for e in range(ne)]
    m1 = logit[0]
    i1 = jnp.zeros(m1.shape, jnp.int32)
    for e in range(1, ne):
        better = logit[e] > m1
        m1 = jnp.where(better, logit[e], m1)
        i1 = jnp.where(better, e, i1)
    m2 = jnp.full(m1.shape, -jnp.inf, F32)
    i2 = jnp.zeros(m1.shape, jnp.int32)
    for e in range(ne):
        cand = jnp.where(i1 == e, -jnp.inf, logit[e])
        better = cand > m2
        m2 = jnp.where(better, cand, m2)
        i2 = jnp.where(better, e, i2)
    e2 = jnp.exp(m2 - m1)
    den = 1.0 + e2
    slot = lax.broadcasted_iota(jnp.int32, (x.shape[0], TOP_K), 1)
    idx_ref[...] = jnp.where(slot == 0, i1, i2)
    w_ref[...] = jnp.where(slot == 0, 1.0 / den, e2 / den)


def _router(x2, nw, router):
    t, d = x2.shape
    tm = ROUTER_TM
    assert t % tm == 0
    return pl.pallas_call(
        _router_kernel,
        grid=(t // tm,),
        in_specs=[pl.BlockSpec((tm, d), lambda i: (i, 0)),
                  _const_spec(nw.shape), _const_spec(router.shape)],
        out_specs=[pl.BlockSpec((tm, d), lambda i: (i, 0)),
                   pl.BlockSpec((tm, TOP_K), lambda i: (i, 0)),
                   pl.BlockSpec((tm, TOP_K), lambda i: (i, 0))],
        out_shape=[jax.ShapeDtypeStruct((t, d), BF16),
                   jax.ShapeDtypeStruct((t, TOP_K), jnp.int32),
                   jax.ShapeDtypeStruct((t, TOP_K), F32)],
        compiler_params=pltpu.CompilerParams(
            dimension_semantics=("arbitrary",), vmem_limit_bytes=VMEM_LIMIT),
        name="moe_router",
    )(x2, nw, router)


def _route(top_i, n_experts, bm, tc):
    t = top_i.shape[0]
    ex = jnp.arange(n_experts, dtype=jnp.int32)
    sel = top_i[:, :, None] == ex[None, None, :]
    chosen = jnp.any(sel, axis=1).astype(jnp.int32)
    csum = jnp.cumsum(chosen, axis=0)
    counts = csum[-1]
    nb_e = (counts + bm - 1) // bm + 1
    blk_end = jnp.cumsum(nb_e)
    blk_start = blk_end - nb_e
    row0 = blk_start * bm
    seli = sel.astype(jnp.int32)
    pos = jnp.sum(seli * (csum - 1 + row0[None, :])[:, None, :], axis=2)

    nb = (t * TOP_K) // bm + 2 * n_experts
    bi = jnp.arange(nb, dtype=jnp.int32)
    blk_e = jnp.minimum(jnp.sum((bi[:, None] >= blk_end[None, :]).astype(jnp.int32), axis=1),
                        n_experts - 1)
    blk_valid = jnp.clip(counts[blk_e] - (bi - blk_start[blk_e]) * bm, 0, bm)
    blk_valid = jnp.where(bi < blk_end[-1], blk_valid, 0)

    tile_end = csum[tc - 1::tc]
    tile_prev = jnp.concatenate([jnp.zeros((1, n_experts), jnp.int32), tile_end[:-1]], axis=0)
    pair_row = row0[None, :] + tile_prev
    pair_cnt = tile_end - tile_prev
    i32 = lambda a: a.astype(jnp.int32)
    return i32(pos), i32(blk_e), i32(blk_valid), i32(pair_row), i32(pair_cnt)


def _dispatch_window_write(wbuf, x_hbm, sem, start, slot, e):
    return pltpu.make_async_copy(wbuf.at[slot, e], x_hbm.at[pl.ds(start, DISPATCH_WIN), :],
                                 sem.at[slot, e])


def _dispatch_kernel(row_ref, cnt_ref, h_ref, idx_ref, pos_ref, zeros_hbm, x_hbm,
                     wbuf, xbuf, carry, sem, xsem):
    del zeros_hbm
    j = pl.program_id(0)
    nt = pl.num_programs(0)
    ne = carry.shape[0]
    win = DISPATCH_WIN
    grp = BF16_SUBLANES
    slot = j % 2

    @pl.when(j == 0)
    def _():
        carry[...] = jnp.zeros_like(carry)

    @pl.when(j > 0)
    def _():
        for e in range(ne):
            _dispatch_window_write(wbuf, x_hbm, sem, 0, 1 - slot, e).wait()

    h = h_ref[...]
    idx = idx_ref[0]
    pos = pos_ref[0]
    rows = lax.broadcasted_iota(jnp.int32, (win, 1), 0)

    def selection(e, start):
        hit = None
        for kk in range(TOP_K):
            off = jnp.where(idx[kk:kk + 1, :] == e, pos[kk:kk + 1, :] - start, -1)
            hk = off == rows
            hit = hk if hit is None else jnp.logical_or(hit, hk)
        return jnp.where(hit, 1.0, 0.0).astype(BF16)

    for e in range(ne):
        row = row_ref[j * ne + e]
        cnt = cnt_ref[j * ne + e]
        a0 = pl.multiple_of((row // grp) * grp, grp)
        fill = row - a0 + cnt
        w = _dot(selection(e, a0), h)
        head = w[0:grp, :] + carry[e].astype(F32)
        wbuf[slot, e, 0:grp, :] = head.astype(BF16)
        wbuf[slot, e, grp:, :] = w[grp:, :].astype(BF16)
        _dispatch_window_write(wbuf, x_hbm, sem, a0, slot, e).start()
        n_more = fill // win

        @pl.when(n_more == 0)
        def _(e=e, fill=fill):
            g = pl.multiple_of((fill // grp) * grp, grp)
            carry[e] = wbuf[slot, e, pl.ds(g, grp), :]

        @pl.when(n_more > 0)
        def _(e=e, a0=a0, fill=fill, n_more=n_more):
            def more(m, c):
                st = pl.multiple_of(a0 + (m + 1) * win, grp)
                xbuf[...] = _dot(selection(e, st), h).astype(BF16)
                cp = pltpu.make_async_copy(xbuf, x_hbm.at[pl.ds(st, win), :], xsem.at[0])
                cp.start()
                cp.wait()
                return c

            lax.fori_loop(0, n_more, more, 0)
            g = pl.multiple_of(((fill - n_more * win) // grp) * grp, grp)
            carry[e] = xbuf[pl.ds(g, grp), :]

    @pl.when(j == nt - 1)
    def _():
        for e in range(ne):
            _dispatch_window_write(wbuf, x_hbm, sem, 0, slot, e).wait()


def _moe_dispatch(h, top_i, pos, pair_row, pair_cnt, zero_rows, tc):
    t, d = h.shape
    n_rows = zero_rows.shape[0]
    nt = t // tc
    ne = pair_row.shape[1]
    lane_major = lambda a: a.reshape(nt, tc, TOP_K).transpose(0, 2, 1)
    grid_spec = pltpu.PrefetchScalarGridSpec(
        num_scalar_prefetch=2,
        grid=(nt,),
        in_specs=[pl.BlockSpec((tc, d), lambda j, a, b: (j, 0)),
                  pl.BlockSpec((1, TOP_K, tc), lambda j, a, b: (j, 0, 0)),
                  pl.BlockSpec((1, TOP_K, tc), lambda j, a, b: (j, 0, 0)),
                  pl.BlockSpec(memory_space=pl.ANY)],
        out_specs=pl.BlockSpec(memory_space=pl.ANY),
        scratch_shapes=[pltpu.VMEM((2, ne, DISPATCH_WIN, d), BF16),
                        pltpu.VMEM((DISPATCH_WIN, d), BF16),
                        pltpu.VMEM((ne, BF16_SUBLANES, d), BF16),
                        pltpu.SemaphoreType.DMA((2, ne)),
                        pltpu.SemaphoreType.DMA((1,))],
    )
    return pl.pallas_call(
        _dispatch_kernel,
        grid_spec=grid_spec,
        out_shape=jax.ShapeDtypeStruct((n_rows, d), BF16),
        input_output_aliases={5: 0},
        compiler_params=pltpu.CompilerParams(
            dimension_semantics=("arbitrary",), vmem_limit_bytes=VMEM_LIMIT),
        name="moe_dispatch",
    )(pair_row.reshape(-1), pair_cnt.reshape(-1), h, lane_major(top_i), lane_major(pos),
      zero_rows)


def _experts_kernel(blk_e_ref, valid_ref, x_ref, wg_ref, wu_ref, wd_ref, o_ref, acc):
    del blk_e_ref
    i = pl.program_id(0)
    f = pl.program_id(1)
    nf = pl.num_programs(1)
    active = valid_ref[i] > 0

    @pl.when(active)
    def _():
        xb = x_ref[...]
        gt = _dot(xb, wg_ref[0])
        up = _dot(xb, wu_ref[0])
        a = (gt * _sigmoid(gt) * up).astype(BF16)
        part = _dot(a, wd_ref[0])

        @pl.when(f == 0)
        def _():
            acc[...] = part

        @pl.when(f > 0)
        def _():
            acc[...] = acc[...] + part

    @pl.when(f == nf - 1)
    def _():
        @pl.when(active)
        def _():
            o_ref[...] = acc[...].astype(o_ref.dtype)

        @pl.when(jnp.logical_not(active))
        def _():
            o_ref[...] = jnp.zeros_like(o_ref)


def _moe_experts(x_sorted, blk_e, blk_valid, wg, wu, wd, bm):
    rows, d = x_sorted.shape
    nb = rows // bm
    tf = MOE_TF
    ff = wg.shape[2]
    assert ff % tf == 0
    nf = ff // tf

    def wcol(i, f, bv):
        return jnp.where(bv[i] > 0, f, nf - 1)

    grid_spec = pltpu.PrefetchScalarGridSpec(
        num_scalar_prefetch=2,
        grid=(nb, nf),
        in_specs=[
            pl.BlockSpec((bm, d), lambda i, f, be, bv: (i, 0)),
            pl.BlockSpec((1, d, tf), lambda i, f, be, bv: (be[i], 0, wcol(i, f, bv))),
            pl.BlockSpec((1, d, tf), lambda i, f, be, bv: (be[i], 0, wcol(i, f, bv))),
            pl.BlockSpec((1, tf, d), lambda i, f, be, bv: (be[i], wcol(i, f, bv), 0)),
        ],
        out_specs=pl.BlockSpec((bm, d), lambda i, f, be, bv: (i, 0)),
        scratch_shapes=[pltpu.VMEM((bm, d), F32)],
    )
    return pl.pallas_call(
        _experts_kernel,
        grid_spec=grid_spec,
        out_shape=jax.ShapeDtypeStruct((rows, d), BF16),
        compiler_params=pltpu.CompilerParams(
            dimension_semantics=("arbitrary", "arbitrary"),
            vmem_limit_bytes=VMEM_LIMIT),
        name="moe_experts",
    )(blk_e, blk_valid, x_sorted, wg, wu, wd)


def _window_start(row, n_rows):
    a0 = (row // BF16_SUBLANES) * BF16_SUBLANES
    return pl.multiple_of(jnp.minimum(a0, n_rows - COMBINE_WIN), BF16_SUBLANES)


def _window_copy(y_hbm, ybuf, sem, start, slot, e):
    return pltpu.make_async_copy(y_hbm.at[pl.ds(start, COMBINE_WIN), :],
                                 ybuf.at[slot, pl.ds(e * COMBINE_WIN, COMBINE_WIN), :],
                                 sem.at[slot, e])


def _combine_kernel(row_ref, cnt_ref, x_ref, idx_ref, w_ref, pos_ref, nw_ref, y_hbm,
                    o_ref, ybuf, mbuf, xtra, zacc, sem, xsem, *, apply_norm):
    j = pl.program_id(0)
    nt = pl.num_programs(0)
    ne = mbuf.shape[1] // COMBINE_WIN
    n_rows = y_hbm.shape[0]
    slot = j % 2

    def start_tile(tile, s):
        for e in range(ne):
            _window_copy(y_hbm, ybuf, sem, _window_start(row_ref[tile * ne + e], n_rows), s, e).start()

    @pl.when(j == 0)
    def _():
        start_tile(0, 0)

    @pl.when(j + 1 < nt)
    def _():
        start_tile(j + 1, 1 - slot)

    idx = idx_ref[...]
    wts = w_ref[...]
    pos = pos_ref[...]
    lanes = lax.broadcasted_iota(jnp.int32, (1, COMBINE_WIN), 1)

    def weights_tile(e, start, skip):
        m = jnp.zeros((idx.shape[0], COMBINE_WIN), F32)
        for kk in range(TOP_K):
            off = pos[:, kk:kk + 1] - start
            valid = jnp.logical_and(idx[:, kk:kk + 1] == e, off >= skip)
            off = jnp.where(valid, off, -1)
            m = jnp.where(off == lanes, wts[:, kk:kk + 1], m)
        return m.astype(BF16)

    def finish(s):
        starts = [_window_start(row_ref[j * ne + e], n_rows) for e in range(ne)]
        for e in range(ne):
            mbuf[:, e * COMBINE_WIN:(e + 1) * COMBINE_WIN] = weights_tile(e, starts[e], 0)
        for e in range(ne):
            _window_copy(y_hbm, ybuf, sem, starts[e], s, e).wait()
        z = x_ref[...] + _dot(mbuf[...], ybuf[s])
        n_more = [jnp.maximum((row_ref[j * ne + e] - starts[e] + cnt_ref[j * ne + e]
                               + COMBINE_WIN - 1) // COMBINE_WIN - 1, 0) for e in range(ne)]
        any_more = functools.reduce(lambda a, b: a + b, n_more)

        def emit(zz):
            o_ref[...] = _rmsnorm(zz, nw_ref[...]) if apply_norm else zz

        @pl.when(any_more == 0)
        def _():
            emit(z)

        @pl.when(any_more > 0)
        def _():
            zacc[...] = z
            for e in range(ne):
                def more(m, carry, e=e, start=starts[e]):
                    want = start + (m + 1) * COMBINE_WIN
                    st = pl.multiple_of(jnp.minimum(want, n_rows - COMBINE_WIN), BF16_SUBLANES)
                    cp = pltpu.make_async_copy(y_hbm.at[pl.ds(st, COMBINE_WIN), :], xtra, xsem.at[0])
                    cp.start()
                    cp.wait()
                    zacc[...] += _dot(weights_tile(e, st, want - st), xtra[...])
                    return carry

                lax.fori_loop(0, n_more[e], more, 0)
            emit(zacc[...])

    finish(slot)


def _moe_combine(x2, y_sorted, top_i, top_w, pos, pair_row, pair_cnt, nw, apply_norm):
    t, d = x2.shape
    tm = MOE_TC
    nt = t // tm
    ne = pair_row.shape[1]
    assert y_sorted.shape[0] >= COMBINE_WIN
    grid_spec = pltpu.PrefetchScalarGridSpec(
        num_scalar_prefetch=2,
        grid=(nt,),
        in_specs=[
            pl.BlockSpec((tm, d), lambda i, a, b: (i, 0)),
            pl.BlockSpec((tm, TOP_K), lambda i, a, b: (i, 0)),
            pl.BlockSpec((tm, TOP_K), lambda i, a, b: (i, 0)),
            pl.BlockSpec((tm, TOP_K), lambda i, a, b: (i, 0)),
            pl.BlockSpec(nw.shape, lambda i, a, b: (0, 0)),
            pl.BlockSpec(memory_space=pl.ANY),
        ],
        out_specs=pl.BlockSpec((tm, d), lambda i, a, b: (i, 0)),
        scratch_shapes=[pltpu.VMEM((2, ne * COMBINE_WIN, d), BF16),
                        pltpu.VMEM((tm, ne * COMBINE_WIN), BF16),
                        pltpu.VMEM((COMBINE_WIN, d), BF16),
                        pltpu.VMEM((tm, d), F32),
                        pltpu.SemaphoreType.DMA((2, ne)),
                        pltpu.SemaphoreType.DMA((1,))],
    )
    return pl.pallas_call(
        functools.partial(_combine_kernel, apply_norm=apply_norm),
        grid_spec=grid_spec,
        out_shape=jax.ShapeDtypeStruct((t, d), F32),
        compiler_params=pltpu.CompilerParams(
            dimension_semantics=("arbitrary",), vmem_limit_bytes=VMEM_LIMIT),
        name="moe_combine",
    )(pair_row.reshape(-1), pair_cnt.reshape(-1), x2, top_i, top_w, pos, nw, y_sorted)


def _final_norm_kernel(x_ref, nw_ref, o_ref):
    o_ref[...] = _rmsnorm(x_ref[...], nw_ref[...])


def _final_norm(x2, nw):
    t, d = x2.shape
    tm = DENSE_TM
    return pl.pallas_call(
        _final_norm_kernel,
        grid=(t // tm,),
        in_specs=[pl.BlockSpec((tm, d), lambda i: (i, 0)), _const_spec(nw.shape)],
        out_specs=pl.BlockSpec((tm, d), lambda i: (i, 0)),
        out_shape=jax.ShapeDtypeStruct((t, d), F32),
        name="final_norm",
    )(x2, nw)


def kernel(x, attn_norm_w, w_in, gate_bias, pool_w, pool_scale, pool_up, hgrn_lower_bounds,
           hgrn_norm_w, hgrn_up, w_out, ffn_norm_w, dense_w_gate, dense_w_up, dense_w_down,
           moe_router, moe_w_gate, moe_w_up, moe_w_down, final_norm_w):
    b, s, d = x.shape
    depth = w_in.shape[0]
    n_experts = moe_router.shape[-1]

    lb_soft = jax.nn.softmax(hgrn_lower_bounds.astype(F32), axis=0)
    lb = jnp.clip(jnp.cumsum(lb_soft, axis=0) - lb_soft[0:1], 0.0, 1.0 - 1e-6)
    lbf = jnp.maximum(lb, LB_FLOOR)
    oml = 1.0 - lb
    dlb = lb - lbf

    def row(a):
        return a.reshape(1, -1).astype(F32)

    final_w = row(final_norm_w)
    n_sorted_rows = (b * s * TOP_K // MOE_BM + 2 * n_experts) * MOE_BM
    normed = None
    for l in range(depth):
        x = _token_mixer(
            x, row(attn_norm_w[l]), w_in[l].astype(BF16), row(gate_bias[l]),
            pool_w[l].astype(BF16), row(pool_scale[l]), pool_up[l].astype(BF16),
            row(lbf[l]), row(oml[l]), row(dlb[l]),
            row(jnp.tile(hgrn_norm_w[l], HGRN_HEADS)), hgrn_up[l].astype(BF16),
            w_out[l].astype(BF16))
        x2 = x.reshape(b * s, d)
        j = l // 2
        last = l == depth - 1
        if l % 2 == 0:
            routed_next = l + 1 < depth
            to_cast = (moe_w_gate[j], moe_w_up[j], moe_w_down[j]) if routed_next else ()
            x2, moe_w16, sorted_zeros = _dense_ffn(
                x2, row(ffn_norm_w[l]), dense_w_gate[j].astype(BF16),
                dense_w_up[j].astype(BF16), dense_w_down[j].astype(BF16),
                to_cast, n_sorted_rows if routed_next else 0)
            if last:
                normed = _final_norm(x2, final_w)
        else:
            h, top_i, top_w = _router(x2, row(ffn_norm_w[l]), moe_router[j].astype(F32).T)
            pos, blk_e, blk_valid, pair_row, pair_cnt = _route(top_i, n_experts, MOE_BM, MOE_TC)
            x_sorted = _moe_dispatch(h, top_i, pos, pair_row, pair_cnt, sorted_zeros, MOE_TC)
            y_sorted = _moe_experts(x_sorted, blk_e, blk_valid, *moe_w16, MOE_BM)
            x2 = _moe_combine(x2, y_sorted, top_i, top_w, pos, pair_row, pair_cnt, final_w,
                              apply_norm=last)
            if last:
                normed = x2
        x = x2.reshape(b, s, d)
    return normed.reshape(b, s, d)
```

```python
import functools

import jax
import jax.numpy as jnp
from jax import lax
from jax.experimental import pallas as pl
from jax.experimental.pallas import tpu as pltpu

F32 = jnp.float32
BF16 = jnp.bfloat16

EPS = 1e-6
LB_FLOOR = 1e-30
POOL_WINDOWS = (2, 4, 8, 16)
N_POOL_GROUPS = 4
HGRN_HEADS = 4
TOP_K = 2
LANE = 128
BF16_SUBLANES = 16

MIXER_TM = 512
HGRN_CHUNK = 256
POOL_CARRY = 16
DENSE_TM = 512
ROUTER_TM = 512
MOE_BM = 512
MOE_TF = 1792
MOE_TC = 512
DISPATCH_WIN = 192
COMBINE_WIN = 256
VMEM_LIMIT = 56 * 1024 * 1024


def _const_spec(shape):
    nd = len(shape)
    return pl.BlockSpec(shape, lambda *_: (0,) * nd, pipeline_mode=pl.Buffered(1))


def _dot(a, b):
    return jnp.dot(a, b, preferred_element_type=F32)


def _dot_nt(a, b):
    return lax.dot_general(a, b, (((1,), (1,)), ((), ())), preferred_element_type=F32)


def _dot_tn(a, b):
    return lax.dot_general(a, b, (((0,), (0,)), ((), ())), preferred_element_type=F32)


def _sigmoid(x):
    return 0.5 * jnp.tanh(0.5 * x) + 0.5


def _rmsnorm(x, w):
    ms = jnp.mean(x * x, axis=-1, keepdims=True)
    return x * lax.rsqrt(ms + EPS) * w


def _hgrn_chunk(q, k, v, c, state_ref, hd, side_work):
    n = q.shape[0]
    rows = lax.broadcasted_iota(jnp.int32, (n, 1), 0)
    cols = lax.broadcasted_iota(jnp.int32, (1, n), 1)
    v16 = v.astype(BF16)
    scores = [jnp.zeros((n, n), F32) for _ in range(HGRN_HEADS)]

    p = c
    tot = c
    half = 1
    shift = 1
    while half < n:
        upper = (rows & half) != 0
        e = jnp.exp2(jnp.where(upper, p, tot - p))
        row_key = jnp.where(upper, rows >> shift, -1)
        col_key = jnp.where((cols & half) != 0, -2, cols >> shift)
        owned = row_key == col_key
        for hi in range(HGRN_HEADS):
            sl = slice(hi * hd, (hi + 1) * hd)
            eh = e[:, sl]
            qt = (q[:, sl] * eh).astype(BF16)
            kt = (k[:, sl] * eh).astype(BF16)
            scores[hi] = jnp.where(owned, _dot_nt(qt, kt), scores[hi])
        other = jnp.where(upper, pltpu.roll(tot, half, 0), pltpu.roll(tot, n - half, 0))
        p = p + jnp.where(upper, other, 0.0)
        tot = tot + other
        if side_work:
            if side_work[0][0] <= 0:
                side_work.pop(0)[1]()
            else:
                side_work[0][0] -= 1
        half *= 2
        shift += 1

    eg = jnp.exp2(p)
    er = jnp.exp2(tot - p)
    o_heads = []
    for hi in range(HGRN_HEADS):
        sl = slice(hi * hd, (hi + 1) * hd)
        qh, kh, vh = q[:, sl], k[:, sl], v[:, sl]
        st = state_ref[hi]
        o = _dot(scores[hi].astype(BF16), v16[:, sl])
        o = o + jnp.sum(qh * kh, axis=-1, keepdims=True) * vh
        o = o + _dot_nt((qh * eg[:, sl]).astype(BF16), st.astype(BF16))
        kd = (kh * er[:, sl]).astype(BF16)
        state_ref[hi] = st * jnp.exp2(tot[0:1, sl]) + _dot_tn(v16[:, sl], kd)
        o = o * lax.rsqrt(jnp.mean(o * o, axis=-1, keepdims=True) + EPS)
        o_heads.append(o)
    return jnp.concatenate(o_heads, axis=1)


def _mixer_kernel(x_ref, nw_ref, win_ref, gb_ref, pw_ref, ps_ref, pup_ref,
                  lbf_ref, oml_ref, dl_ref, hnw_ref, hup_ref, wout_ref,
                  o_ref, carry_ref, state_ref):
    tm = x_ref.shape[1]
    d = x_ref.shape[2]
    pw_cols = pup_ref.shape[0]
    hw = hup_ref.shape[0]
    gd = pw_cols // N_POOL_GROUPS
    hd = hw // HGRN_HEADS
    off_q = pw_cols
    off_f = off_q + hw
    off_i = off_f + hw
    off_g = off_i + hw
    off_gates = off_g + hw
    si = pl.program_id(1)

    @pl.when(si == 0)
    def _():
        carry_ref[...] = jnp.zeros_like(carry_ref)
        state_ref[...] = jnp.zeros_like(state_ref)

    x = x_ref[0]
    h = _rmsnorm(x, nw_ref[...]).astype(BF16)

    def pool_branch():
        u = _dot(h, win_ref[:, 0:pw_cols])
        ext = jnp.concatenate([carry_ref[...], u], axis=0)
        carry_ref[...] = u[tm - POOL_CARRY:, :]
        pos = (si * tm + 1 + lax.broadcasted_iota(jnp.int32, (tm, 1), 0)).astype(F32)
        acc = ext
        span = 1
        mixed = []
        for gi, w in enumerate(POOL_WINDOWS):
            while span < w:
                acc = acc + pltpu.roll(acc, span, 0)
                span *= 2
            wsum = acc[POOL_CARRY:, 0:gd]
            acc = acc[:, gd:] if gi + 1 < N_POOL_GROUPS else None
            ug = u[:, gi * gd:(gi + 1) * gd]
            pooled = wsum * (1.0 / jnp.minimum(pos, float(w))) - ug
            mixed.append(_dot(pooled.astype(BF16), pw_ref[gi]))
        mixed = jnp.concatenate(mixed, axis=1) * ps_ref[...]
        return _dot(mixed.astype(BF16), pup_ref[...])

    q_pre = _dot(h, win_ref[:, off_q:off_f])
    q = q_pre * _sigmoid(q_pre)
    fp = _dot(h, win_ref[:, off_f:off_i])
    v = _dot(h, win_ref[:, off_i:off_g])
    g_pre = _dot(h, win_ref[:, off_g:off_gates])
    g = g_pre * _sigmoid(g_pre)

    en = jnp.exp(-jnp.abs(fp))
    rden = 1.0 / (1.0 + en)
    nonneg = fp >= 0
    spos = jnp.where(nonneg, 1.0, en) * rden
    sneg = jnp.where(nonneg, en, 1.0) * rden
    c = jnp.log2(lbf_ref[...] + oml_ref[...] * spos)
    k = oml_ref[...] * sneg + dl_ref[...]

    gz_parts = []
    y_pool_out = []
    gate_cols = 2 * LANE
    side_work = [
        [0, functools.partial(
            lambda c0: gz_parts.append(_dot(h, win_ref[:, off_gates + c0:off_gates + c0 + gate_cols])),
            c0)]
        for c0 in range(0, 2 * d, gate_cols)]
    side_work.insert(len(side_work) // 2, [0, lambda: y_pool_out.append(pool_branch())])
    ck = HGRN_CHUNK
    o_chunks = [_hgrn_chunk(q[r0:r0 + ck], k[r0:r0 + ck], v[r0:r0 + ck], c[r0:r0 + ck],
                            state_ref, hd, side_work)
                for r0 in range(0, tm, ck)]
    while side_work:
        side_work.pop(0)[1]()
    o_all = jnp.concatenate(o_chunks, axis=0) * hnw_ref[...] * g
    y_hgrn = _dot(o_all.astype(BF16), hup_ref[...])

    gz = jnp.concatenate(gz_parts, axis=1) + gb_ref[...]
    gates = _sigmoid(gz)
    merged = gates[:, 0:d] * y_pool_out[0] + gates[:, d:2 * d] * y_hgrn
    o_ref[0] = x + _dot(merged.astype(BF16), wout_ref[...])


def _token_mixer(x, nw, win, gb, pw, ps, pup, lbf, oml, dl, hnw, hup, wout):
    b, s, d = x.shape
    tm = MIXER_TM
    pw_cols = pup.shape[0]
    hw = hup.shape[0]
    hd = hw // HGRN_HEADS
    assert s % tm == 0 and tm >= POOL_CARRY
    consts = (nw, win, gb, pw, ps, pup, lbf, oml, dl, hnw, hup, wout)
    return pl.pallas_call(
        _mixer_kernel,
        grid=(b, s // tm),
        in_specs=[pl.BlockSpec((1, tm, d), lambda bi, si: (bi, si, 0))]
        + [_const_spec(a.shape) for a in consts],
        out_specs=pl.BlockSpec((1, tm, d), lambda bi, si: (bi, si, 0)),
        out_shape=jax.ShapeDtypeStruct(x.shape, F32),
        scratch_shapes=[pltpu.VMEM((POOL_CARRY, pw_cols), F32),
                        pltpu.VMEM((HGRN_HEADS, hd, hd), F32)],
        compiler_params=pltpu.CompilerParams(
            dimension_semantics=("arbitrary", "arbitrary"),
            vmem_limit_bytes=VMEM_LIMIT),
        name="token_mixer",
    )(x, *consts)


def _dense_ffn_kernel(x_ref, nw_ref, wg_ref, wu_ref, wd_ref, *rest, n_chunks, n_cast):
    cast_in, o_ref = rest[:n_cast], rest[n_cast]
    cast_out, zero_out = rest[n_cast + 1:2 * n_cast + 1], rest[2 * n_cast + 1:]
    x = x_ref[...]
    h = _rmsnorm(x, nw_ref[...]).astype(BF16)
    ff = wg_ref.shape[1]
    fc = ff // n_chunks
    out = x
    for ci in range(n_chunks):
        sl = slice(ci * fc, (ci + 1) * fc)
        gt = _dot(h, wg_ref[:, sl])
        up = _dot(h, wu_ref[:, sl])
        a = (gt * _sigmoid(gt) * up).astype(BF16)
        out = out + _dot(a, wd_ref[sl, :])
    o_ref[...] = out
    for src, dst in zip(cast_in, cast_out):
        dst[...] = src[...].astype(dst.dtype)
    for z in zero_out:
        z[...] = jnp.zeros_like(z)


def _dense_ffn(x2, nw, wg, wu, wd, to_cast=(), zero_rows=0):
    t, d = x2.shape
    tm = DENSE_TM
    ff = wg.shape[1]
    n_chunks = 2 if ff % (2 * LANE) == 0 else 1
    assert t % tm == 0
    n = t // tm
    slabs = []
    for a in to_cast:
        rows = a.size // a.shape[-1]
        assert rows % (n * BF16_SUBLANES) == 0
        slabs.append(a.reshape(n, rows // n, a.shape[-1]))
    slab_specs = [pl.BlockSpec((1,) + s.shape[1:], lambda i: (i, 0, 0)) for s in slabs]
    zero_specs, zero_shapes = [], []
    if zero_rows:
        assert zero_rows % (n * BF16_SUBLANES) == 0
        zero_specs = [pl.BlockSpec((zero_rows // n, d), lambda i: (i, 0))]
        zero_shapes = [jax.ShapeDtypeStruct((zero_rows, d), BF16)]
    outs = pl.pallas_call(
        functools.partial(_dense_ffn_kernel, n_chunks=n_chunks, n_cast=len(slabs)),
        grid=(n,),
        in_specs=[pl.BlockSpec((tm, d), lambda i: (i, 0)),
                  _const_spec(nw.shape), _const_spec(wg.shape),
                  _const_spec(wu.shape), _const_spec(wd.shape)] + slab_specs,
        out_specs=[pl.BlockSpec((tm, d), lambda i: (i, 0))] + slab_specs + zero_specs,
        out_shape=[jax.ShapeDtypeStruct(x2.shape, F32)]
        + [jax.ShapeDtypeStruct(s.shape, BF16) for s in slabs] + zero_shapes,
        compiler_params=pltpu.CompilerParams(
            dimension_semantics=("arbitrary",), vmem_limit_bytes=VMEM_LIMIT),
        name="dense_ffn",
    )(x2, nw, wg, wu, wd, *slabs)
    casted = [o.reshape(a.shape) for o, a in zip(outs[1:1 + len(slabs)], to_cast)]
    return outs[0], casted, (outs[-1] if zero_rows else None)


def _router_kernel(x_ref, nw_ref, r_ref, h_ref, idx_ref, w_ref):
    x = x_ref[...]
    h = _rmsnorm(x, nw_ref[...])
    h_ref[...] = h.astype(BF16)
    ne = r_ref.shape[0]
    logit = [jnp.sum(h * r_ref[e:e + 1, :], axis=-1, keepdims=True) for e in range(ne)]
    m1 = logit[0]
    i1 = jnp.zeros(m1.shape, jnp.int32)
    for e in range(1, ne):
        better = logit[e] > m1
        m1 = jnp.where(better, logit[e], m1)
        i1 = jnp.where(better, e, i1)
    m2 = jnp.full(m1.shape, -jnp.inf, F32)
    i2 = jnp.zeros(m1.shape, jnp.int32)
    for e in range(ne):
        cand = jnp.where(i1 == e, -jnp.inf, logit[e])
        better = cand > m2
        m2 = jnp.where(better, cand, m2)
        i2 = jnp.where(better, e, i2)
    e2 = jnp.exp(m2 - m1)
    den = 1.0 + e2
    slot = lax.broadcasted_iota(jnp.int32, (x.shape[0], TOP_K), 1)
    idx_ref[...] = jnp.where(slot == 0, i1, i2)
    w_ref[...] = jnp.where(slot == 0, 1.0 / den, e2 / den)


def _router(x2, nw, router):
    t, d = x2.shape
    tm = ROUTER_TM
    assert t % tm == 0
    return pl.pallas_call(
        _router_kernel,
        grid=(t // tm,),
        in_specs=[pl.BlockSpec((tm, d), lambda i: (i, 0)),
                  _const_spec(nw.shape), _const_spec(router.shape)],
        out_specs=[pl.BlockSpec((tm, d), lambda i: (i, 0)),
                   pl.BlockSpec((tm, TOP_K), lambda i: (i, 0)),
                   pl.BlockSpec((tm, TOP_K), lambda i: (i, 0))],
        out_shape=[jax.ShapeDtypeStruct((t, d), BF16),
                   jax.ShapeDtypeStruct((t, TOP_K), jnp.int32),
                   jax.ShapeDtypeStruct((t, TOP_K), F32)],
        compiler_params=pltpu.CompilerParams(
            dimension_semantics=("arbitrary",), vmem_limit_bytes=VMEM_LIMIT),
        name="moe_router",
    )(x2, nw, router)


def _route(top_i, n_experts, bm, tc):
    t = top_i.shape[0]
    ex = jnp.arange(n_experts, dtype=jnp.int32)
    sel = top_i[:, :, None] == ex[None, None, :]
    chosen = jnp.any(sel, axis=1).astype(jnp.int32)
    csum = jnp.cumsum(chosen, axis=0)
    counts = csum[-1]
    nb_e = (counts + bm - 1) // bm + 1
    blk_end = jnp.cumsum(nb_e)
    blk_start = blk_end - nb_e
    row0 = blk_start * bm
    seli = sel.astype(jnp.int32)
    pos = jnp.sum(seli * (csum - 1 + row0[None, :])[:, None, :], axis=2)

    nb = (t * TOP_K) // bm + 2 * n_experts
    bi = jnp.arange(nb, dtype=jnp.int32)
    blk_e = jnp.minimum(jnp.sum((bi[:, None] >= blk_end[None, :]).astype(jnp.int32), axis=1),
                        n_experts - 1)
    blk_valid = jnp.clip(counts[blk_e] - (bi - blk_start[blk_e]) * bm, 0, bm)
    blk_valid = jnp.where(bi < blk_end[-1], blk_valid, 0)

    tile_end = csum[tc - 1::tc]
    tile_prev = jnp.concatenate([jnp.zeros((1, n_experts), jnp.int32), tile_end[:-1]], axis=0)
    pair_row = row0[None, :] + tile_prev
    pair_cnt = tile_end - tile_prev
    i32 = lambda a: a.astype(jnp.int32)
    return i32(pos), i32(blk_e), i32(blk_valid), i32(pair_row), i32(pair_cnt)


def _dispatch_window_write(wbuf, x_hbm, sem, start, slot, e):
    return pltpu.make_async_copy(wbuf.at[slot, e], x_hbm.at[pl.ds(start, DISPATCH_WIN), :],
                                 sem.at[slot, e])


def _dispatch_kernel(row_ref, cnt_ref, h_ref, idx_ref, pos_ref, zeros_hbm, x_hbm,
                     wbuf, xbuf, carry, sem, xsem):
    del zeros_hbm
    j = pl.program_id(0)
    nt = pl.num_programs(0)
    ne = carry.shape[0]
    win = DISPATCH_WIN
    grp = BF16_SUBLANES
    slot = j % 2

    @pl.when(j == 0)
    def _():
        carry[...] = jnp.zeros_like(carry)

    @pl.when(j > 0)
    def _():
        for e in range(ne):
            _dispatch_window_write(wbuf, x_hbm, sem, 0, 1 - slot, e).wait()

    h = h_ref[...]
    idx = idx_ref[0]
    pos = pos_ref[0]
    rows = lax.broadcasted_iota(jnp.int32, (win, 1), 0)

    def selection(e, start):
        hit = None
        for kk in range(TOP_K):
            off = jnp.where(idx[kk:kk + 1, :] == e, pos[kk:kk + 1, :] - start, -1)
            hk = off == rows
            hit = hk if hit is None else jnp.logical_or(hit, hk)
        return jnp.where(hit, 1.0, 0.0).astype(BF16)

    for e in range(ne):
        row = row_ref[j * ne + e]
        cnt = cnt_ref[j * ne + e]
        a0 = pl.multiple_of((row // grp) * grp, grp)
        fill = row - a0 + cnt
        w = _dot(selection(e, a0), h)
        head = w[0:grp, :] + carry[e].astype(F32)
        wbuf[slot, e, 0:grp, :] = head.astype(BF16)
        wbuf[slot, e, grp:, :] = w[grp:, :].astype(BF16)
        _dispatch_window_write(wbuf, x_hbm, sem, a0, slot, e).start()
        n_more = fill // win

        @pl.when(n_more == 0)
        def _(e=e, fill=fill):
            g = pl.multiple_of((fill // grp) * grp, grp)
            carry[e] = wbuf[slot, e, pl.ds(g, grp), :]

        @pl.when(n_more > 0)
        def _(e=e, a0=a0, fill=fill, n_more=n_more):
            def more(m, c):
                st = pl.multiple_of(a0 + (m + 1) * win, grp)
                xbuf[...] = _dot(selection(e, st), h).astype(BF16)
                cp = pltpu.make_async_copy(xbuf, x_hbm.at[pl.ds(st, win), :], xsem.at[0])
                cp.start()
                cp.wait()
                return c

            lax.fori_loop(0, n_more, more, 0)
            g = pl.multiple_of(((fill - n_more * win) // grp) * grp, grp)
            carry[e] = xbuf[pl.ds(g, grp), :]

    @pl.when(j == nt - 1)
    def _():
        for e in range(ne):
            _dispatch_window_write(wbuf, x_hbm, sem, 0, slot, e).wait()


def _moe_dispatch(h, top_i, pos, pair_row, pair_cnt, zero_rows, tc):
    t, d = h.shape
    n_rows = zero_rows.shape[0]
    nt = t // tc
    ne = pair_row.shape[1]
    lane_major = lambda a: a.reshape(nt, tc, TOP_K).transpose(0, 2, 1)
    grid_spec = pltpu.PrefetchScalarGridSpec(
        num_scalar_prefetch=2,
        grid=(nt,),
        in_specs=[pl.BlockSpec((tc, d), lambda j, a, b: (j, 0)),
                  pl.BlockSpec((1, TOP_K, tc), lambda j, a, b: (j, 0, 0)),
                  pl.BlockSpec((1, TOP_K, tc), lambda j, a, b: (j, 0, 0)),
                  pl.BlockSpec(memory_space=pl.ANY)],
        out_specs=pl.BlockSpec(memory_space=pl.ANY),
        scratch_shapes=[pltpu.VMEM((2, ne, DISPATCH_WIN, d), BF16),
                        pltpu.VMEM((DISPATCH_WIN, d), BF16),
                        pltpu.VMEM((ne, BF16_SUBLANES, d), BF16),
                        pltpu.SemaphoreType.DMA((2, ne)),
                        pltpu.SemaphoreType.DMA((1,))],
    )
    return pl.pallas_call(
        _dispatch_kernel,
        grid_spec=grid_spec,
        out_shape=jax.ShapeDtypeStruct((n_rows, d), BF16),
        input_output_aliases={5: 0},
        compiler_params=pltpu.CompilerParams(
            dimension_semantics=("arbitrary",), vmem_limit_bytes=VMEM_LIMIT),
        name="moe_dispatch",
    )(pair_row.reshape(-1), pair_cnt.reshape(-1), h, lane_major(top_i), lane_major(pos),
      zero_rows)


def _experts_kernel(blk_e_ref, valid_ref, x_ref, wg_ref, wu_ref, wd_ref, o_ref, acc):
    del blk_e_ref
    i = pl.program_id(0)
    f = pl.program_id(1)
    nf = pl.num_programs(1)
    active = valid_ref[i] > 0
    first = f == 0
    final = f == nf - 1

    def partial_out():
        xb = x_ref[...]
        gt = _dot(xb, wg_ref[0])
        up = _dot(xb, wu_ref[0])
        a = (gt * _sigmoid(gt) * up).astype(BF16)
        return _dot(a, wd_ref[0])

    @pl.when(active & first & final)
    def _():
        o_ref[...] = partial_out().astype(o_ref.dtype)

    @pl.when(active & first & jnp.logical_not(final))
    def _():
        acc[...] = partial_out()

    @pl.when(active & jnp.logical_not(first) & jnp.logical_not(final))
    def _():
        acc[...] += partial_out()

    @pl.when(active & jnp.logical_not(first) & final)
    def _():
        o_ref[...] = (acc[...] + partial_out()).astype(o_ref.dtype)

    @pl.when(jnp.logical_not(active) & final)
    def _():
        o_ref[...] = jnp.zeros_like(o_ref)


def _moe_experts(x_sorted, blk_e, blk_valid, wg, wu, wd, bm):
    rows, d = x_sorted.shape
    nb = rows // bm
    tf = MOE_TF
    ff = wg.shape[2]
    assert ff % tf == 0
    nf = ff // tf

    def wcol(i, f, bv):
        return jnp.where(bv[i] > 0, f, nf - 1)

    grid_spec = pltpu.PrefetchScalarGridSpec(
        num_scalar_prefetch=2,
        grid=(nb, nf),
        in_specs=[
            pl.BlockSpec((bm, d), lambda i, f, be, bv: (i, 0)),
            pl.BlockSpec((1, d, tf), lambda i, f, be, bv: (be[i], 0, wcol(i, f, bv))),
            pl.BlockSpec((1, d, tf), lambda i, f, be, bv: (be[i], 0, wcol(i, f, bv))),
            pl.BlockSpec((1, tf, d), lambda i, f, be, bv: (be[i], wcol(i, f, bv), 0)),
        ],
        out_specs=pl.BlockSpec((bm, d), lambda i, f, be, bv: (i, 0)),
        scratch_shapes=[pltpu.VMEM((bm, d), F32)],
    )
    return pl.pallas_call(
        _experts_kernel,
        grid_spec=grid_spec,
        out_shape=jax.ShapeDtypeStruct((rows, d), BF16),
        compiler_params=pltpu.CompilerParams(
            dimension_semantics=("arbitrary", "arbitrary"),
            vmem_limit_bytes=VMEM_LIMIT),
        name="moe_experts",
    )(blk_e, blk_valid, x_sorted, wg, wu, wd)


def _window_start(row, n_rows):
    a0 = (row // BF16_SUBLANES) * BF16_SUBLANES
    return pl.multiple_of(jnp.minimum(a0, n_rows - COMBINE_WIN), BF16_SUBLANES)


def _window_copy(y_hbm, ybuf, sem, start, slot, e):
    return pltpu.make_async_copy(y_hbm.at[pl.ds(start, COMBINE_WIN), :],
                                 ybuf.at[slot, pl.ds(e * COMBINE_WIN, COMBINE_WIN), :],
                                 sem.at[slot, e])


def _combine_kernel(row_ref, cnt_ref, x_ref, idx_ref, w_ref, pos_ref, nw_ref, y_hbm,
                    o_ref, ybuf, xtra, zacc, sem, xsem, *, apply_norm):
    j = pl.program_id(0)
    nt = pl.num_programs(0)
    ne = ybuf.shape[1] // COMBINE_WIN
    n_rows = y_hbm.shape[0]
    slot = j % 2

    def start_tile(tile, s):
        for e in range(ne):
            _window_copy(y_hbm, ybuf, sem, _window_start(row_ref[tile * ne + e], n_rows), s, e).start()

    @pl.when(j == 0)
    def _():
        start_tile(0, 0)

    @pl.when(j + 1 < nt)
    def _():
        start_tile(j + 1, 1 - slot)

    idx = idx_ref[...]
    wts = w_ref[...]
    pos = pos_ref[...]
    lanes = lax.broadcasted_iota(jnp.int32, (1, COMBINE_WIN), 1)

    def weights_tile(e, start, skip):
        m = jnp.zeros((idx.shape[0], COMBINE_WIN), F32)
        for kk in range(TOP_K):
            off = pos[:, kk:kk + 1] - start
            valid = jnp.logical_and(idx[:, kk:kk + 1] == e, off >= skip)
            off = jnp.where(valid, off, -1)
            m = jnp.where(off == lanes, wts[:, kk:kk + 1], m)
        return m.astype(BF16)

    def finish(s):
        starts = [_window_start(row_ref[j * ne + e], n_rows) for e in range(ne)]
        for e in range(ne):
            _window_copy(y_hbm, ybuf, sem, starts[e], s, e).wait()
        z = x_ref[...]
        for e in range(ne):
            z = z + _dot(weights_tile(e, starts[e], 0),
                         ybuf[s, e * COMBINE_WIN:(e + 1) * COMBINE_WIN, :])
        n_more = [jnp.maximum((row_ref[j * ne + e] - starts[e] + cnt_ref[j * ne + e]
                               + COMBINE_WIN - 1) // COMBINE_WIN - 1, 0) for e in range(ne)]
        any_more = functools.reduce(lambda a, b: a + b, n_more)

        def emit(zz):
            o_ref[...] = _rmsnorm(zz, nw_ref[...]) if apply_norm else zz

        @pl.when(any_more == 0)
        def _():
            emit(z)

        @pl.when(any_more > 0)
        def _():
            zacc[...] = z
            for e in range(ne):
                def more(m, carry, e=e, start=starts[e]):
                    want = start + (m + 1) * COMBINE_WIN
                    st = pl.multiple_of(jnp.minimum(want, n_rows - COMBINE_WIN), BF16_SUBLANES)
                    cp = pltpu.make_async_copy(y_hbm.at[pl.ds(st, COMBINE_WIN), :], xtra, xsem.at[0])
                    cp.start()
                    cp.wait()
                    zacc[...] += _dot(weights_tile(e, st, want - st), xtra[...])
                    return carry

                lax.fori_loop(0, n_more[e], more, 0)
            emit(zacc[...])

    finish(slot)


def _moe_combine(x2, y_sorted, top_i, top_w, pos, pair_row, pair_cnt, nw, apply_norm):
    t, d = x2.shape
    tm = MOE_TC
    nt = t // tm
    ne = pair_row.shape[1]
    assert y_sorted.shape[0] >= COMBINE_WIN
    grid_spec = pltpu.PrefetchScalarGridSpec(
        num_scalar_prefetch=2,
        grid=(nt,),
        in_specs=[
            pl.BlockSpec((tm, d), lambda i, a, b: (i, 0)),
            pl.BlockSpec((tm, TOP_K), lambda i, a, b: (i, 0)),
            pl.BlockSpec((tm, TOP_K), lambda i, a, b: (i, 0)),
            pl.BlockSpec((tm, TOP_K), lambda i, a, b: (i, 0)),
            pl.BlockSpec(nw.shape, lambda i, a, b: (0, 0)),
            pl.BlockSpec(memory_space=pl.ANY),
        ],
        out_specs=pl.BlockSpec((tm, d), lambda i, a, b: (i, 0)),
        scratch_shapes=[pltpu.VMEM((2, ne * COMBINE_WIN, d), BF16),
                        pltpu.VMEM((COMBINE_WIN, d), BF16),
                        pltpu.VMEM((tm, d), F32),
                        pltpu.SemaphoreType.DMA((2, ne)),
                        pltpu.SemaphoreType.DMA((1,))],
    )
    return pl.pallas_call(
        functools.partial(_combine_kernel, apply_norm=apply_norm),
        grid_spec=grid_spec,
        out_shape=jax.ShapeDtypeStruct((t, d), F32),
        compiler_params=pltpu.CompilerParams(
            dimension_semantics=("arbitrary",), vmem_limit_bytes=VMEM_LIMIT),
        name="moe_combine",
    )(pair_row.reshape(-1), pair_cnt.reshape(-1), x2, top_i, top_w, pos, nw, y_sorted)


def _final_norm_kernel(x_ref, nw_ref, o_ref):
    o_ref[...] = _rmsnorm(x_ref[...], nw_ref[...])


def _final_norm(x2, nw):
    t, d = x2.shape
    tm = DENSE_TM
    return pl.pallas_call(
        _final_norm_kernel,
        grid=(t // tm,),
        in_specs=[pl.BlockSpec((tm, d), lambda i: (i, 0)), _const_spec(nw.shape)],
        out_specs=pl.BlockSpec((tm, d), lambda i: (i, 0)),
        out_shape=jax.ShapeDtypeStruct((t, d), F32),
        name="final_norm",
    )(x2, nw)


def kernel(x, attn_norm_w, w_in, gate_bias, pool_w, pool_scale, pool_up, hgrn_lower_bounds,
           hgrn_norm_w, hgrn_up, w_out, ffn_norm_w, dense_w_gate, dense_w_up, dense_w_down,
           moe_router, moe_w_gate, moe_w_up, moe_w_down, final_norm_w):
    b, s, d = x.shape
    depth = w_in.shape[0]
    n_experts = moe_router.shape[-1]

    lb_soft = jax.nn.softmax(hgrn_lower_bounds.astype(F32), axis=0)
    lb = jnp.clip(jnp.cumsum(lb_soft, axis=0) - lb_soft[0:1], 0.0, 1.0 - 1e-6)
    lbf = jnp.maximum(lb, LB_FLOOR)
    oml = 1.0 - lb
    dlb = lb - lbf

    def row(a):
        return a.reshape(1, -1).astype(F32)

    final_w = row(final_norm_w)
    n_sorted_rows = (b * s * TOP_K // MOE_BM + 2 * n_experts) * MOE_BM
    normed = None
    for l in range(depth):
        x = _token_mixer(
            x, row(attn_norm_w[l]), w_in[l].astype(BF16), row(gate_bias[l]),
            pool_w[l].astype(BF16), row(pool_scale[l]), pool_up[l].astype(BF16),
            row(lbf[l]), row(oml[l]), row(dlb[l]),
            row(jnp.tile(hgrn_norm_w[l], HGRN_HEADS)), hgrn_up[l].astype(BF16),
            w_out[l].astype(BF16))
        x2 = x.reshape(b * s, d)
        j = l // 2
        last = l == depth - 1
        if l % 2 == 0:
            routed_next = l + 1 < depth
            to_cast = (moe_w_gate[j], moe_w_up[j], moe_w_down[j]) if routed_next else ()
            x2, moe_w16, sorted_zeros = _dense_ffn(
                x2, row(ffn_norm_w[l]), dense_w_gate[j].astype(BF16),
                dense_w_up[j].astype(BF16), dense_w_down[j].astype(BF16),
                to_cast, n_sorted_rows if routed_next else 0)
            if last:
                normed = _final_norm(x2, final_w)
        else:
            h, top_i, top_w = _router(x2, row(ffn_norm_w[l]), moe_router[j].astype(F32).T)
            pos, blk_e, blk_valid, pair_row, pair_cnt = _route(top_i, n_experts, MOE_BM, MOE_TC)
            x_sorted = _moe_dispatch(h, top_i, pos, pair_row, pair_cnt, sorted_zeros, MOE_TC)
            y_sorted = _moe_experts(x_sorted, blk_e, blk_valid, *moe_w16, MOE_BM)
            x2 = _moe_combine(x2, y_sorted, top_i, top_w, pos, pair_row, pair_cnt, final_w,
                              apply_norm=last)
            if last:
                normed = x2
        x = x2.reshape(b, s, d)
    return normed.reshape(b, s, d)
```

```python
import functools

import jax
import jax.numpy as jnp
from jax import lax
from jax.experimental import pallas as pl
from jax.experimental.pallas import tpu as pltpu

F32 = jnp.float32
BF16 = jnp.bfloat16

EPS = 1e-6
LB_FLOOR = 1e-30
POOL_WINDOWS = (2, 4, 8, 16)
N_POOL_GROUPS = 4
HGRN_HEADS = 4
TOP_K = 2
LANE = 128
BF16_SUBLANES = 16

MIXER_TM = 512
HGRN_CHUNK = 256
POOL_CARRY = 16
DENSE_TM = 512
MOE_BM = 512
MOE_TF = 1792
MOE_TC = 512
ROUTER_TM = MOE_TC
DISPATCH_WIN = 192
COMBINE_WIN = 256
VMEM_LIMIT = 56 * 1024 * 1024


def _const_spec(shape):
    nd = len(shape)
    return pl.BlockSpec(shape, lambda *_: (0,) * nd, pipeline_mode=pl.Buffered(1))


def _dot(a, b):
    return jnp.dot(a, b, preferred_element_type=F32)


def _dot_nt(a, b):
    return lax.dot_general(a, b, (((1,), (1,)), ((), ())), preferred_element_type=F32)


def _dot_tn(a, b):
    return lax.dot_general(a, b, (((0,), (0,)), ((), ())), preferred_element_type=F32)


def _sigmoid(x):
    return 0.5 * jnp.tanh(0.5 * x) + 0.5


def _rmsnorm(x, w):
    ms = jnp.mean(x * x, axis=-1, keepdims=True)
    return x * lax.rsqrt(ms + EPS) * w


def _hgrn_chunk(q, k, v, c, state_ref, hd, side_work):
    n = q.shape[0]
    rows = lax.broadcasted_iota(jnp.int32, (n, 1), 0)
    cols = lax.broadcasted_iota(jnp.int32, (1, n), 1)
    v16 = v.astype(BF16)
    scores = [jnp.zeros((n, n), F32) for _ in range(HGRN_HEADS)]

    p = c
    tot = c
    half = 1
    shift = 1
    while half < n:
        upper = (rows & half) != 0
        e = jnp.exp2(jnp.where(upper, p, tot - p))
        row_key = jnp.where(upper, rows >> shift, -1)
        col_key = jnp.where((cols & half) != 0, -2, cols >> shift)
        owned = row_key == col_key
        for hi in range(HGRN_HEADS):
            sl = slice(hi * hd, (hi + 1) * hd)
            eh = e[:, sl]
            qt = (q[:, sl] * eh).astype(BF16)
            kt = (k[:, sl] * eh).astype(BF16)
            scores[hi] = jnp.where(owned, _dot_nt(qt, kt), scores[hi])
        other = jnp.where(upper, pltpu.roll(tot, half, 0), pltpu.roll(tot, n - half, 0))
        p = p + jnp.where(upper, other, 0.0)
        tot = tot + other
        if side_work:
            if side_work[0][0] <= 0:
                side_work.pop(0)[1]()
            else:
                side_work[0][0] -= 1
        half *= 2
        shift += 1

    eg = jnp.exp2(p)
    er = jnp.exp2(tot - p)
    o_heads = []
    for hi in range(HGRN_HEADS):
        sl = slice(hi * hd, (hi + 1) * hd)
        qh, kh, vh = q[:, sl], k[:, sl], v[:, sl]
        st = state_ref[hi]
        o = _dot(scores[hi].astype(BF16), v16[:, sl])
        o = o + jnp.sum(qh * kh, axis=-1, keepdims=True) * vh
        o = o + _dot_nt((qh * eg[:, sl]).astype(BF16), st.astype(BF16))
        kd = (kh * er[:, sl]).astype(BF16)
        state_ref[hi] = st * jnp.exp2(tot[0:1, sl]) + _dot_tn(v16[:, sl], kd)
        o = o * lax.rsqrt(jnp.mean(o * o, axis=-1, keepdims=True) + EPS)
        o_heads.append(o)
    return jnp.concatenate(o_heads, axis=1)


def _mixer_kernel(x_ref, nw_ref, win_ref, gb_ref, pw_ref, ps_ref, pup_ref,
                  lbf_ref, oml_ref, dl_ref, hnw_ref, hup_ref, wout_ref,
                  o_ref, carry_ref, state_ref):
    tm = x_ref.shape[1]
    d = x_ref.shape[2]
    pw_cols = pup_ref.shape[0]
    hw = hup_ref.shape[0]
    gd = pw_cols // N_POOL_GROUPS
    hd = hw // HGRN_HEADS
    off_q = pw_cols
    off_f = off_q + hw
    off_i = off_f + hw
    off_g = off_i + hw
    off_gates = off_g + hw
    si = pl.program_id(1)

    @pl.when(si == 0)
    def _():
        carry_ref[...] = jnp.zeros_like(carry_ref)
        state_ref[...] = jnp.zeros_like(state_ref)

    x = x_ref[0]
    h = _rmsnorm(x, nw_ref[...]).astype(BF16)

    def pool_branch():
        u = _dot(h, win_ref[:, 0:pw_cols])
        ext = jnp.concatenate([carry_ref[...], u], axis=0)
        carry_ref[...] = u[tm - POOL_CARRY:, :]
        pos = (si * tm + 1 + lax.broadcasted_iota(jnp.int32, (tm, 1), 0)).astype(F32)
        acc = ext
        span = 1
        mixed = []
        for gi, w in enumerate(POOL_WINDOWS):
            while span < w:
                acc = acc + pltpu.roll(acc, span, 0)
                span *= 2
            wsum = acc[POOL_CARRY:, 0:gd]
            acc = acc[:, gd:] if gi + 1 < N_POOL_GROUPS else None
            ug = u[:, gi * gd:(gi + 1) * gd]
            pooled = wsum * (1.0 / jnp.minimum(pos, float(w))) - ug
            mixed.append(_dot(pooled.astype(BF16), pw_ref[gi]))
        mixed = jnp.concatenate(mixed, axis=1) * ps_ref[...]
        return _dot(mixed.astype(BF16), pup_ref[...])

    q_pre = _dot(h, win_ref[:, off_q:off_f])
    q = q_pre * _sigmoid(q_pre)
    fp = _dot(h, win_ref[:, off_f:off_i])
    v = _dot(h, win_ref[:, off_i:off_g])
    g_pre = _dot(h, win_ref[:, off_g:off_gates])
    g = g_pre * _sigmoid(g_pre)

    en = jnp.exp(-jnp.abs(fp))
    rden = 1.0 / (1.0 + en)
    nonneg = fp >= 0
    spos = jnp.where(nonneg, 1.0, en) * rden
    sneg = jnp.where(nonneg, en, 1.0) * rden
    c = jnp.log2(lbf_ref[...] + oml_ref[...] * spos)
    k = oml_ref[...] * sneg + dl_ref[...]

    gz_parts = []
    y_pool_out = []
    gate_cols = 2 * LANE
    side_work = [
        [0, functools.partial(
            lambda c0: gz_parts.append(_dot(h, win_ref[:, off_gates + c0:off_gates + c0 + gate_cols])),
            c0)]
        for c0 in range(0, 2 * d, gate_cols)]
    side_work.insert(len(side_work) // 2, [0, lambda: y_pool_out.append(pool_branch())])
    ck = HGRN_CHUNK
    o_chunks = [_hgrn_chunk(q[r0:r0 + ck], k[r0:r0 + ck], v[r0:r0 + ck], c[r0:r0 + ck],
                            state_ref, hd, side_work)
                for r0 in range(0, tm, ck)]
    while side_work:
        side_work.pop(0)[1]()
    o_all = jnp.concatenate(o_chunks, axis=0) * hnw_ref[...] * g
    y_hgrn = _dot(o_all.astype(BF16), hup_ref[...])

    gz = jnp.concatenate(gz_parts, axis=1) + gb_ref[...]
    gates = _sigmoid(gz)
    merged = gates[:, 0:d] * y_pool_out[0] + gates[:, d:2 * d] * y_hgrn
    o_ref[0] = x + _dot(merged.astype(BF16), wout_ref[...])


def _token_mixer(x, nw, win, gb, pw, ps, pup, lbf, oml, dl, hnw, hup, wout):
    b, s, d = x.shape
    tm = MIXER_TM
    pw_cols = pup.shape[0]
    hw = hup.shape[0]
    hd = hw // HGRN_HEADS
    assert s % tm == 0 and tm >= POOL_CARRY
    consts = (nw, win, gb, pw, ps, pup, lbf, oml, dl, hnw, hup, wout)
    return pl.pallas_call(
        _mixer_kernel,
        grid=(b, s // tm),
        in_specs=[pl.BlockSpec((1, tm, d), lambda bi, si: (bi, si, 0))]
        + [_const_spec(a.shape) for a in consts],
        out_specs=pl.BlockSpec((1, tm, d), lambda bi, si: (bi, si, 0)),
        out_shape=jax.ShapeDtypeStruct(x.shape, F32),
        scratch_shapes=[pltpu.VMEM((POOL_CARRY, pw_cols), F32),
                        pltpu.VMEM((HGRN_HEADS, hd, hd), F32)],
        compiler_params=pltpu.CompilerParams(
            dimension_semantics=("arbitrary", "arbitrary"),
            vmem_limit_bytes=VMEM_LIMIT),
        name="token_mixer",
    )(x, *consts)


def _dense_ffn_kernel(x_ref, nw_ref, wg_ref, wu_ref, wd_ref, *rest, n_chunks, n_cast):
    cast_in, o_ref = rest[:n_cast], rest[n_cast]
    cast_out, zero_out = rest[n_cast + 1:2 * n_cast + 1], rest[2 * n_cast + 1:]
    x = x_ref[...]
    h = _rmsnorm(x, nw_ref[...]).astype(BF16)
    ff = wg_ref.shape[1]
    fc = ff // n_chunks
    out = x
    for ci in range(n_chunks):
        sl = slice(ci * fc, (ci + 1) * fc)
        gt = _dot(h, wg_ref[:, sl])
        up = _dot(h, wu_ref[:, sl])
        a = (gt * _sigmoid(gt) * up).astype(BF16)
        out = out + _dot(a, wd_ref[sl, :])
    o_ref[...] = out
    for src, dst in zip(cast_in, cast_out):
        dst[...] = src[...].astype(dst.dtype)
    for z in zero_out:
        z[...] = jnp.zeros_like(z)


def _dense_ffn(x2, nw, wg, wu, wd, to_cast=(), zero_rows=0):
    t, d = x2.shape
    tm = DENSE_TM
    ff = wg.shape[1]
    n_chunks = 2 if ff % (2 * LANE) == 0 else 1
    assert t % tm == 0
    n = t // tm
    slabs = []
    for a in to_cast:
        rows = a.size // a.shape[-1]
        assert rows % (n * BF16_SUBLANES) == 0
        slabs.append(a.reshape(n, rows // n, a.shape[-1]))
    slab_specs = [pl.BlockSpec((1,) + s.shape[1:], lambda i: (i, 0, 0)) for s in slabs]
    zero_specs, zero_shapes = [], []
    if zero_rows:
        assert zero_rows % (n * BF16_SUBLANES) == 0
        zero_specs = [pl.BlockSpec((zero_rows // n, d), lambda i: (i, 0))]
        zero_shapes = [jax.ShapeDtypeStruct((zero_rows, d), BF16)]
    outs = pl.pallas_call(
        functools.partial(_dense_ffn_kernel, n_chunks=n_chunks, n_cast=len(slabs)),
        grid=(n,),
        in_specs=[pl.BlockSpec((tm, d), lambda i: (i, 0)),
                  _const_spec(nw.shape), _const_spec(wg.shape),
                  _const_spec(wu.shape), _const_spec(wd.shape)] + slab_specs,
        out_specs=[pl.BlockSpec((tm, d), lambda i: (i, 0))] + slab_specs + zero_specs,
        out_shape=[jax.ShapeDtypeStruct(x2.shape, F32)]
        + [jax.ShapeDtypeStruct(s.shape, BF16) for s in slabs] + zero_shapes,
        compiler_params=pltpu.CompilerParams(
            dimension_semantics=("arbitrary",), vmem_limit_bytes=VMEM_LIMIT),
        name="dense_ffn",
    )(x2, nw, wg, wu, wd, *slabs)
    casted = [o.reshape(a.shape) for o, a in zip(outs[1:1 + len(slabs)], to_cast)]
    return outs[0], casted, (outs[-1] if zero_rows else None)


def _router_kernel(x_ref, nw_ref, r_ref, tri_ref, h_ref, idx_ref, w_ref, rank_ref, cnt_ref):
    x = x_ref[...]
    h = _rmsnorm(x, nw_ref[...])
    h_ref[...] = h.astype(BF16)
    ne = r_ref.shape[0]
    logit = [jnp.sum(h * r_ref[e:e + 1, :], axis=-1, keepdims=True) for e in range(ne)]
    m1 = logit[0]
    i1 = jnp.zeros(m1.shape, jnp.int32)
    for e in range(1, ne):
        better = logit[e] > m1
        m1 = jnp.where(better, logit[e], m1)
        i1 = jnp.where(better, e, i1)
    m2 = jnp.full(m1.shape, -jnp.inf, F32)
    i2 = jnp.zeros(m1.shape, jnp.int32)
    for e in range(ne):
        cand = jnp.where(i1 == e, -jnp.inf, logit[e])
        better = cand > m2
        m2 = jnp.where(better, cand, m2)
        i2 = jnp.where(better, e, i2)
    e2 = jnp.exp(m2 - m1)
    den = 1.0 + e2
    slot = lax.broadcasted_iota(jnp.int32, (x.shape[0], TOP_K), 1)
    idx_ref[...] = jnp.where(slot == 0, i1, i2)
    w_ref[...] = jnp.where(slot == 0, 1.0 / den, e2 / den)
    lanes = lax.broadcasted_iota(jnp.int32, (1, LANE), 1)
    chosen = jnp.where(jnp.logical_or(lanes == i1, lanes == i2), 1.0, 0.0).astype(BF16)
    csum = _dot(tri_ref[...], chosen)
    rank1 = jnp.sum(jnp.where(lanes == i1, csum, 0.0), axis=-1, keepdims=True)
    rank2 = jnp.sum(jnp.where(lanes == i2, csum, 0.0), axis=-1, keepdims=True)
    rank_ref[...] = (jnp.where(slot == 0, rank1, rank2) - 1.0).astype(jnp.int32)
    tm = x.shape[0]
    cnt_ref[0] = csum[tm - 1:tm, :].astype(jnp.int32)


def _router(x2, nw, router):
    t, d = x2.shape
    tm = ROUTER_TM
    ne = router.shape[0]
    assert t % tm == 0 and ne <= LANE
    nt = t // tm
    tri = jnp.tril(jnp.ones((tm, tm), BF16))
    h, top_i, top_w, rank, cnt = pl.pallas_call(
        _router_kernel,
        grid=(nt,),
        in_specs=[pl.BlockSpec((tm, d), lambda i: (i, 0)),
                  _const_spec(nw.shape), _const_spec(router.shape), _const_spec(tri.shape)],
        out_specs=[pl.BlockSpec((tm, d), lambda i: (i, 0)),
                   pl.BlockSpec((tm, TOP_K), lambda i: (i, 0)),
                   pl.BlockSpec((tm, TOP_K), lambda i: (i, 0)),
                   pl.BlockSpec((tm, TOP_K), lambda i: (i, 0)),
                   pl.BlockSpec((1, 1, LANE), lambda i: (i, 0, 0))],
        out_shape=[jax.ShapeDtypeStruct((t, d), BF16),
                   jax.ShapeDtypeStruct((t, TOP_K), jnp.int32),
                   jax.ShapeDtypeStruct((t, TOP_K), F32),
                   jax.ShapeDtypeStruct((t, TOP_K), jnp.int32),
                   jax.ShapeDtypeStruct((nt, 1, LANE), jnp.int32)],
        compiler_params=pltpu.CompilerParams(
            dimension_semantics=("arbitrary",), vmem_limit_bytes=VMEM_LIMIT),
        name="moe_router",
    )(x2, nw, router, tri)
    return h, top_i, top_w, rank, cnt[:, 0, :ne]


def _route(tile_cnt, bm, tc):
    nt, n_experts = tile_cnt.shape
    counts = jnp.sum(tile_cnt, axis=0)
    nb_e = (counts + bm - 1) // bm + 1
    blk_end = jnp.cumsum(nb_e)
    blk_start = blk_end - nb_e
    row0 = blk_start * bm

    nb = (nt * tc * TOP_K) // bm + 2 * n_experts
    bi = jnp.arange(nb, dtype=jnp.int32)
    blk_e = jnp.minimum(jnp.sum((bi[:, None] >= blk_end[None, :]).astype(jnp.int32), axis=1),
                        n_experts - 1)
    blk_valid = jnp.clip(counts[blk_e] - (bi - blk_start[blk_e]) * bm, 0, bm)
    blk_valid = jnp.where(bi < blk_end[-1], blk_valid, 0)

    tile_prev = jnp.cumsum(tile_cnt, axis=0) - tile_cnt
    pair_row = row0[None, :] + tile_prev
    i32 = lambda a: a.astype(jnp.int32)
    return i32(blk_e), i32(blk_valid), i32(pair_row), i32(tile_cnt)


def _dispatch_window_write(wbuf, x_hbm, sem, start, slot, e):
    return pltpu.make_async_copy(wbuf.at[slot, e], x_hbm.at[pl.ds(start, DISPATCH_WIN), :],
                                 sem.at[slot, e])


def _dispatch_kernel(row_ref, cnt_ref, h_ref, idx_ref, rank_ref, zeros_hbm, x_hbm,
                     wbuf, xbuf, carry, sem, xsem):
    del zeros_hbm
    j = pl.program_id(0)
    nt = pl.num_programs(0)
    ne = carry.shape[0]
    win = DISPATCH_WIN
    grp = BF16_SUBLANES
    slot = j % 2

    @pl.when(j == 0)
    def _():
        carry[...] = jnp.zeros_like(carry)

    @pl.when(j > 0)
    def _():
        for e in range(ne):
            _dispatch_window_write(wbuf, x_hbm, sem, 0, 1 - slot, e).wait()

    h = h_ref[...]
    idx = idx_ref[0]
    rank = rank_ref[0]
    rows = lax.broadcasted_iota(jnp.int32, (win, 1), 0)

    def selection(e, lead):
        hit = None
        for kk in range(TOP_K):
            off = jnp.where(idx[kk:kk + 1, :] == e, rank[kk:kk + 1, :] + lead, -1)
            hk = off == rows
            hit = hk if hit is None else jnp.logical_or(hit, hk)
        return jnp.where(hit, 1.0, 0.0).astype(BF16)

    a0s, fills, n_mores = [], [], []
    for e in range(ne):
        row = row_ref[j * ne + e]
        cnt = cnt_ref[j * ne + e]
        a0 = pl.multiple_of((row // grp) * grp, grp)
        fill = row - a0 + cnt
        w = _dot(selection(e, row - a0), h)
        head = w[0:grp, :] + carry[e].astype(F32)
        wbuf[slot, e, 0:grp, :] = head.astype(BF16)
        wbuf[slot, e, grp:, :] = w[grp:, :].astype(BF16)
        _dispatch_window_write(wbuf, x_hbm, sem, a0, slot, e).start()
        g = pl.multiple_of(jnp.minimum((fill // grp) * grp, win - grp), grp)
        carry[e] = wbuf[slot, e, pl.ds(g, grp), :]
        a0s.append(a0)
        fills.append(fill)
        n_mores.append(fill // win)

    @pl.when(functools.reduce(lambda a, b: a + b, n_mores) > 0)
    def _():
        for e in range(ne):
            a0, fill, n_more = a0s[e], fills[e], n_mores[e]

            @pl.when(n_more > 0)
            def _(e=e, a0=a0, fill=fill, n_more=n_more):
                lead0 = row_ref[j * ne + e] - a0

                def more(m, c):
                    st = pl.multiple_of(a0 + (m + 1) * win, grp)
                    xbuf[...] = _dot(selection(e, lead0 - (m + 1) * win), h).astype(BF16)
                    cp = pltpu.make_async_copy(xbuf, x_hbm.at[pl.ds(st, win), :], xsem.at[0])
                    cp.start()
                    cp.wait()
                    return c

                lax.fori_loop(0, n_more, more, 0)
                g = pl.multiple_of(((fill - n_more * win) // grp) * grp, grp)
                carry[e] = xbuf[pl.ds(g, grp), :]

    @pl.when(j == nt - 1)
    def _():
        for e in range(ne):
            _dispatch_window_write(wbuf, x_hbm, sem, 0, slot, e).wait()


def _moe_dispatch(h, top_i, rank, pair_row, pair_cnt, zero_rows, tc):
    t, d = h.shape
    n_rows = zero_rows.shape[0]
    nt = t // tc
    ne = pair_row.shape[1]
    lane_major = lambda a: a.reshape(nt, tc, TOP_K).transpose(0, 2, 1)
    grid_spec = pltpu.PrefetchScalarGridSpec(
        num_scalar_prefetch=2,
        grid=(nt,),
        in_specs=[pl.BlockSpec((tc, d), lambda j, a, b: (j, 0)),
                  pl.BlockSpec((1, TOP_K, tc), lambda j, a, b: (j, 0, 0)),
                  pl.BlockSpec((1, TOP_K, tc), lambda j, a, b: (j, 0, 0)),
                  pl.BlockSpec(memory_space=pl.ANY)],
        out_specs=pl.BlockSpec(memory_space=pl.ANY),
        scratch_shapes=[pltpu.VMEM((2, ne, DISPATCH_WIN, d), BF16),
                        pltpu.VMEM((DISPATCH_WIN, d), BF16),
                        pltpu.VMEM((ne, BF16_SUBLANES, d), BF16),
                        pltpu.SemaphoreType.DMA((2, ne)),
                        pltpu.SemaphoreType.DMA((1,))],
    )
    return pl.pallas_call(
        _dispatch_kernel,
        grid_spec=grid_spec,
        out_shape=jax.ShapeDtypeStruct((n_rows, d), BF16),
        input_output_aliases={5: 0},
        compiler_params=pltpu.CompilerParams(
            dimension_semantics=("arbitrary",), vmem_limit_bytes=VMEM_LIMIT),
        name="moe_dispatch",
    )(pair_row.reshape(-1), pair_cnt.reshape(-1), h, lane_major(top_i), lane_major(rank),
      zero_rows)


def _experts_kernel(blk_e_ref, valid_ref, x_ref, wg_ref, wu_ref, wd_ref, o_ref, acc):
    del blk_e_ref
    i = pl.program_id(0)
    f = pl.program_id(1)
    nf = pl.num_programs(1)
    active = valid_ref[i] > 0
    first = f == 0
    final = f == nf - 1

    def partial_out():
        xb = x_ref[...]
        gt = _dot(xb, wg_ref[0])
        up = _dot(xb, wu_ref[0])
        a = (gt * _sigmoid(gt) * up).astype(BF16)
        return _dot(a, wd_ref[0])

    @pl.when(active & first & final)
    def _():
        o_ref[...] = partial_out().astype(o_ref.dtype)

    @pl.when(active & first & jnp.logical_not(final))
    def _():
        acc[...] = partial_out()

    @pl.when(active & jnp.logical_not(first) & jnp.logical_not(final))
    def _():
        acc[...] += partial_out()

    @pl.when(active & jnp.logical_not(first) & final)
    def _():
        o_ref[...] = (acc[...] + partial_out()).astype(o_ref.dtype)

    @pl.when(jnp.logical_not(active) & final)
    def _():
        o_ref[...] = jnp.zeros_like(o_ref)


def _moe_experts(x_sorted, blk_e, blk_valid, wg, wu, wd, bm):
    rows, d = x_sorted.shape
    nb = rows // bm
    tf = MOE_TF
    ff = wg.shape[2]
    assert ff % tf == 0
    nf = ff // tf

    def wcol(i, f, bv):
        return jnp.where(bv[i] > 0, f, nf - 1)

    grid_spec = pltpu.PrefetchScalarGridSpec(
        num_scalar_prefetch=2,
        grid=(nb, nf),
        in_specs=[
            pl.BlockSpec((bm, d), lambda i, f, be, bv: (i, 0)),
            pl.BlockSpec((1, d, tf), lambda i, f, be, bv: (be[i], 0, wcol(i, f, bv))),
            pl.BlockSpec((1, d, tf), lambda i, f, be, bv: (be[i], 0, wcol(i, f, bv))),
            pl.BlockSpec((1, tf, d), lambda i, f, be, bv: (be[i], wcol(i, f, bv), 0)),
        ],
        out_specs=pl.BlockSpec((bm, d), lambda i, f, be, bv: (i, 0)),
        scratch_shapes=[pltpu.VMEM((bm, d), F32)],
    )
    return pl.pallas_call(
        _experts_kernel,
        grid_spec=grid_spec,
        out_shape=jax.ShapeDtypeStruct((rows, d), BF16),
        compiler_params=pltpu.CompilerParams(
            dimension_semantics=("arbitrary", "arbitrary"),
            vmem_limit_bytes=VMEM_LIMIT),
        name="moe_experts",
    )(blk_e, blk_valid, x_sorted, wg, wu, wd)


def _window_start(row, n_rows):
    a0 = (row // BF16_SUBLANES) * BF16_SUBLANES
    return pl.multiple_of(jnp.minimum(a0, n_rows - COMBINE_WIN), BF16_SUBLANES)


def _window_copy(y_hbm, ybuf, sem, start, slot, e):
    return pltpu.make_async_copy(y_hbm.at[pl.ds(start, COMBINE_WIN), :],
                                 ybuf.at[slot, pl.ds(e * COMBINE_WIN, COMBINE_WIN), :],
                                 sem.at[slot, e])


def _combine_kernel(row_ref, cnt_ref, x_ref, idx_ref, w_ref, rank_ref, nw_ref, y_hbm,
                    o_ref, ybuf, xtra, zacc, sem, xsem, *, apply_norm):
    j = pl.program_id(0)
    nt = pl.num_programs(0)
    ne = ybuf.shape[1] // COMBINE_WIN
    n_rows = y_hbm.shape[0]
    slot = j % 2

    def start_tile(tile, s):
        for e in range(ne):
            _window_copy(y_hbm, ybuf, sem, _window_start(row_ref[tile * ne + e], n_rows), s, e).start()

    @pl.when(j == 0)
    def _():
        start_tile(0, 0)

    @pl.when(j + 1 < nt)
    def _():
        start_tile(j + 1, 1 - slot)

    idx = idx_ref[...]
    wts = w_ref[...]
    rank = rank_ref[...]
    lanes = lax.broadcasted_iota(jnp.int32, (1, COMBINE_WIN), 1)

    def weights_tile(e, start, skip):
        m = jnp.zeros((idx.shape[0], COMBINE_WIN), F32)
        lead = row_ref[j * ne + e] - start
        for kk in range(TOP_K):
            off = rank[:, kk:kk + 1] + lead
            valid = jnp.logical_and(idx[:, kk:kk + 1] == e, off >= skip)
            off = jnp.where(valid, off, -1)
            m = jnp.where(off == lanes, wts[:, kk:kk + 1], m)
        return m.astype(BF16)

    def finish(s):
        starts = [_window_start(row_ref[j * ne + e], n_rows) for e in range(ne)]
        for e in range(ne):
            _window_copy(y_hbm, ybuf, sem, starts[e], s, e).wait()
        z = x_ref[...]
        for e in range(ne):
            z = z + _dot(weights_tile(e, starts[e], 0),
                         ybuf[s, e * COMBINE_WIN:(e + 1) * COMBINE_WIN, :])
        n_more = [jnp.maximum((row_ref[j * ne + e] - starts[e] + cnt_ref[j * ne + e]
                               + COMBINE_WIN - 1) // COMBINE_WIN - 1, 0) for e in range(ne)]
        any_more = functools.reduce(lambda a, b: a + b, n_more)

        def emit(zz):
            o_ref[...] = _rmsnorm(zz, nw_ref[...]) if apply_norm else zz

        @pl.when(any_more == 0)
        def _():
            emit(z)

        @pl.when(any_more > 0)
        def _():
            zacc[...] = z
            for e in range(ne):
                def more(m, carry, e=e, start=starts[e]):
                    want = start + (m + 1) * COMBINE_WIN
                    st = pl.multiple_of(jnp.minimum(want, n_rows - COMBINE_WIN), BF16_SUBLANES)
                    cp = pltpu.make_async_copy(y_hbm.at[pl.ds(st, COMBINE_WIN), :], xtra, xsem.at[0])
                    cp.start()
                    cp.wait()
                    zacc[...] += _dot(weights_tile(e, st, want - st), xtra[...])
                    return carry

                lax.fori_loop(0, n_more[e], more, 0)
            emit(zacc[...])

    finish(slot)


def _moe_combine(x2, y_sorted, top_i, top_w, rank, pair_row, pair_cnt, nw, apply_norm):
    t, d = x2.shape
    tm = MOE_TC
    nt = t // tm
    ne = pair_row.shape[1]
    assert y_sorted.shape[0] >= COMBINE_WIN
    grid_spec = pltpu.PrefetchScalarGridSpec(
        num_scalar_prefetch=2,
        grid=(nt,),
        in_specs=[
            pl.BlockSpec((tm, d), lambda i, a, b: (i, 0)),
            pl.BlockSpec((tm, TOP_K), lambda i, a, b: (i, 0)),
            pl.BlockSpec((tm, TOP_K), lambda i, a, b: (i, 0)),
            pl.BlockSpec((tm, TOP_K), lambda i, a, b: (i, 0)),
            pl.BlockSpec(nw.shape, lambda i, a, b: (0, 0)),
            pl.BlockSpec(memory_space=pl.ANY),
        ],
        out_specs=pl.BlockSpec((tm, d), lambda i, a, b: (i, 0)),
        scratch_shapes=[pltpu.VMEM((2, ne * COMBINE_WIN, d), BF16),
                        pltpu.VMEM((COMBINE_WIN, d), BF16),
                        pltpu.VMEM((tm, d), F32),
                        pltpu.SemaphoreType.DMA((2, ne)),
                        pltpu.SemaphoreType.DMA((1,))],
    )
    return pl.pallas_call(
        functools.partial(_combine_kernel, apply_norm=apply_norm),
        grid_spec=grid_spec,
        out_shape=jax.ShapeDtypeStruct((t, d), F32),
        compiler_params=pltpu.CompilerParams(
            dimension_semantics=("arbitrary",), vmem_limit_bytes=VMEM_LIMIT),
        name="moe_combine",
    )(pair_row.reshape(-1), pair_cnt.reshape(-1), x2, top_i, top_w, rank, nw, y_sorted)


def _final_norm_kernel(x_ref, nw_ref, o_ref):
    o_ref[...] = _rmsnorm(x_ref[...], nw_ref[...])


def _final_norm(x2, nw):
    t, d = x2.shape
    tm = DENSE_TM
    return pl.pallas_call(
        _final_norm_kernel,
        grid=(t // tm,),
        in_specs=[pl.BlockSpec((tm, d), lambda i: (i, 0)), _const_spec(nw.shape)],
        out_specs=pl.BlockSpec((tm, d), lambda i: (i, 0)),
        out_shape=jax.ShapeDtypeStruct((t, d), F32),
        name="final_norm",
    )(x2, nw)


def kernel(x, attn_norm_w, w_in, gate_bias, pool_w, pool_scale, pool_up, hgrn_lower_bounds,
           hgrn_norm_w, hgrn_up, w_out, ffn_norm_w, dense_w_gate, dense_w_up, dense_w_down,
           moe_router, moe_w_gate, moe_w_up, moe_w_down, final_norm_w):
    b, s, d = x.shape
    depth = w_in.shape[0]
    n_experts = moe_router.shape[-1]

    lb_soft = jax.nn.softmax(hgrn_lower_bounds.astype(F32), axis=0)
    lb = jnp.clip(jnp.cumsum(lb_soft, axis=0) - lb_soft[0:1], 0.0, 1.0 - 1e-6)
    lbf = jnp.maximum(lb, LB_FLOOR)
    oml = 1.0 - lb
    dlb = lb - lbf

    def row(a):
        return a.reshape(1, -1).astype(F32)

    final_w = row(final_norm_w)
    n_sorted_rows = (b * s * TOP_K // MOE_BM + 2 * n_experts) * MOE_BM
    normed = None
    for l in range(depth):
        x = _token_mixer(
            x, row(attn_norm_w[l]), w_in[l].astype(BF16), row(gate_bias[l]),
            pool_w[l].astype(BF16), row(pool_scale[l]), pool_up[l].astype(BF16),
            row(lbf[l]), row(oml[l]), row(dlb[l]),
            row(jnp.tile(hgrn_norm_w[l], HGRN_HEADS)), hgrn_up[l].astype(BF16),
            w_out[l].astype(BF16))
        x2 = x.reshape(b * s, d)
        j = l // 2
        last = l == depth - 1
        if l % 2 == 0:
            routed_next = l + 1 < depth
            to_cast = (moe_w_gate[j], moe_w_up[j], moe_w_down[j]) if routed_next else ()
            x2, moe_w16, sorted_zeros = _dense_ffn(
                x2, row(ffn_norm_w[l]), dense_w_gate[j].astype(BF16),
                dense_w_up[j].astype(BF16), dense_w_down[j].astype(BF16),
                to_cast, n_sorted_rows if routed_next else 0)
            if last:
                normed = _final_norm(x2, final_w)
        else:
            h, top_i, top_w, rank, tile_cnt = _router(x2, row(ffn_norm_w[l]),
                                                      moe_router[j].astype(F32).T)
            blk_e, blk_valid, pair_row, pair_cnt = _route(tile_cnt, MOE_BM, MOE_TC)
            x_sorted = _moe_dispatch(h, top_i, rank, pair_row, pair_cnt, sorted_zeros, MOE_TC)
            y_sorted = _moe_experts(x_sorted, blk_e, blk_valid, *moe_w16, MOE_BM)
            x2 = _moe_combine(x2, y_sorted, top_i, top_w, rank, pair_row, pair_cnt, final_w,
                              apply_norm=last)
            if last:
                normed = x2
        x = x2.reshape(b, s, d)
    return normed.reshape(b, s, d)
```

```python
import functools

import jax
import jax.numpy as jnp
from jax import lax
from jax.experimental import pallas as pl
from jax.experimental.pallas import tpu as pltpu

F32 = jnp.float32
BF16 = jnp.bfloat16

EPS = 1e-6
LB_FLOOR = 1e-30
POOL_WINDOWS = (2, 4, 8, 16)
N_POOL_GROUPS = 4
HGRN_HEADS = 4
TOP_K = 2
LANE = 128
BF16_SUBLANES = 16

MIXER_TM = 512
HGRN_CHUNK = 256
POOL_CARRY = 16
DENSE_TM = 512
MOE_BM = 512
MOE_TF = 1792
MOE_TC = 512
DISPATCH_WIN = 192
COMBINE_WIN = 256
VMEM_LIMIT = 56 * 1024 * 1024


def _const_spec(shape):
    nd = len(shape)
    return pl.BlockSpec(shape, lambda *_: (0,) * nd, pipeline_mode=pl.Buffered(1))


def _dot(a, b):
    return jnp.dot(a, b, preferred_element_type=F32)


def _dot_nt(a, b):
    return lax.dot_general(a, b, (((1,), (1,)), ((), ())), preferred_element_type=F32)


def _dot_tn(a, b):
    return lax.dot_general(a, b, (((0,), (0,)), ((), ())), preferred_element_type=F32)


def _sigmoid(x):
    return 0.5 * jnp.tanh(0.5 * x) + 0.5


def _rmsnorm(x, w):
    ms = jnp.mean(x * x, axis=-1, keepdims=True)
    return x * lax.rsqrt(ms + EPS) * w


def _hgrn_chunk(q, k, v, c, state_ref, hd, side_work):
    n = q.shape[0]
    rows = lax.broadcasted_iota(jnp.int32, (n, 1), 0)
    cols = lax.broadcasted_iota(jnp.int32, (1, n), 1)
    v16 = v.astype(BF16)
    scores = [jnp.zeros((n, n), F32) for _ in range(HGRN_HEADS)]

    p = c
    tot = c
    half = 1
    shift = 1
    while half < n:
        upper = (rows & half) != 0
        e = jnp.exp2(jnp.where(upper, p, tot - p))
        row_key = jnp.where(upper, rows >> shift, -1)
        col_key = jnp.where((cols & half) != 0, -2, cols >> shift)
        owned = row_key == col_key
        for hi in range(HGRN_HEADS):
            sl = slice(hi * hd, (hi + 1) * hd)
            eh = e[:, sl]
            qt = (q[:, sl] * eh).astype(BF16)
            kt = (k[:, sl] * eh).astype(BF16)
            scores[hi] = jnp.where(owned, _dot_nt(qt, kt), scores[hi])
        other = jnp.where(upper, pltpu.roll(tot, half, 0), pltpu.roll(tot, n - half, 0))
        p = p + jnp.where(upper, other, 0.0)
        tot = tot + other
        if side_work:
            if side_work[0][0] <= 0:
                side_work.pop(0)[1]()
            else:
                side_work[0][0] -= 1
        half *= 2
        shift += 1

    eg = jnp.exp2(p)
    er = jnp.exp2(tot - p)
    o_heads = []
    for hi in range(HGRN_HEADS):
        sl = slice(hi * hd, (hi + 1) * hd)
        qh, kh, vh = q[:, sl], k[:, sl], v[:, sl]
        st = state_ref[hi]
        o = _dot(scores[hi].astype(BF16), v16[:, sl])
        o = o + jnp.sum(qh * kh, axis=-1, keepdims=True) * vh
        o = o + _dot_nt((qh * eg[:, sl]).astype(BF16), st.astype(BF16))
        kd = (kh * er[:, sl]).astype(BF16)
        state_ref[hi] = st * jnp.exp2(tot[0:1, sl]) + _dot_tn(v16[:, sl], kd)
        o = o * lax.rsqrt(jnp.mean(o * o, axis=-1, keepdims=True) + EPS)
        o_heads.append(o)
    return jnp.concatenate(o_heads, axis=1)


def _mixer_kernel(x_ref, nw_ref, win_ref, gb_ref, pw_ref, ps_ref, pup_ref,
                  lbf_ref, oml_ref, dl_ref, hnw_ref, hup_ref, wout_ref, *rest, route):
    if route:
        (fnw_ref, r_ref, tri_ref, o_ref, h_ref, idx_ref, w_ref, rank_ref, cnt_ref,
         carry_ref, state_ref) = rest
    else:
        o_ref, carry_ref, state_ref = rest
    tm = x_ref.shape[1]
    d = x_ref.shape[2]
    pw_cols = pup_ref.shape[0]
    hw = hup_ref.shape[0]
    gd = pw_cols // N_POOL_GROUPS
    hd = hw // HGRN_HEADS
    off_q = pw_cols
    off_f = off_q + hw
    off_i = off_f + hw
    off_g = off_i + hw
    off_gates = off_g + hw
    si = pl.program_id(1)

    @pl.when(si == 0)
    def _():
        carry_ref[...] = jnp.zeros_like(carry_ref)
        state_ref[...] = jnp.zeros_like(state_ref)

    x = x_ref[0]
    h = _rmsnorm(x, nw_ref[...]).astype(BF16)

    def pool_branch():
        u = _dot(h, win_ref[:, 0:pw_cols])
        ext = jnp.concatenate([carry_ref[...], u], axis=0)
        carry_ref[...] = u[tm - POOL_CARRY:, :]
        pos = (si * tm + 1 + lax.broadcasted_iota(jnp.int32, (tm, 1), 0)).astype(F32)
        acc = ext
        span = 1
        mixed = []
        for gi, w in enumerate(POOL_WINDOWS):
            while span < w:
                acc = acc + pltpu.roll(acc, span, 0)
                span *= 2
            wsum = acc[POOL_CARRY:, 0:gd]
            acc = acc[:, gd:] if gi + 1 < N_POOL_GROUPS else None
            ug = u[:, gi * gd:(gi + 1) * gd]
            pooled = wsum * (1.0 / jnp.minimum(pos, float(w))) - ug
            mixed.append(_dot(pooled.astype(BF16), pw_ref[gi]))
        mixed = jnp.concatenate(mixed, axis=1) * ps_ref[...]
        return _dot(mixed.astype(BF16), pup_ref[...])

    q_pre = _dot(h, win_ref[:, off_q:off_f])
    q = q_pre * _sigmoid(q_pre)
    fp = _dot(h, win_ref[:, off_f:off_i])
    v = _dot(h, win_ref[:, off_i:off_g])
    g_pre = _dot(h, win_ref[:, off_g:off_gates])
    g = g_pre * _sigmoid(g_pre)

    en = jnp.exp(-jnp.abs(fp))
    rden = 1.0 / (1.0 + en)
    nonneg = fp >= 0
    spos = jnp.where(nonneg, 1.0, en) * rden
    sneg = jnp.where(nonneg, en, 1.0) * rden
    c = jnp.log2(lbf_ref[...] + oml_ref[...] * spos)
    k = oml_ref[...] * sneg + dl_ref[...]

    gz_parts = []
    y_pool_out = []
    gate_cols = 2 * LANE
    side_work = [
        [0, functools.partial(
            lambda c0: gz_parts.append(_dot(h, win_ref[:, off_gates + c0:off_gates + c0 + gate_cols])),
            c0)]
        for c0 in range(0, 2 * d, gate_cols)]
    side_work.insert(len(side_work) // 2, [0, lambda: y_pool_out.append(pool_branch())])
    ck = HGRN_CHUNK
    o_chunks = [_hgrn_chunk(q[r0:r0 + ck], k[r0:r0 + ck], v[r0:r0 + ck], c[r0:r0 + ck],
                            state_ref, hd, side_work)
                for r0 in range(0, tm, ck)]
    while side_work:
        side_work.pop(0)[1]()
    o_all = jnp.concatenate(o_chunks, axis=0) * hnw_ref[...] * g
    y_hgrn = _dot(o_all.astype(BF16), hup_ref[...])

    gz = jnp.concatenate(gz_parts, axis=1) + gb_ref[...]
    gates = _sigmoid(gz)
    merged = gates[:, 0:d] * y_pool_out[0] + gates[:, d:2 * d] * y_hgrn
    x_new = x + _dot(merged.astype(BF16), wout_ref[...])
    o_ref[0] = x_new
    if route:
        h_ref[0], idx_ref[0], w_ref[0], rank_ref[0], cnt_ref[0, 0] = _route_tile(
            x_new, fnw_ref, r_ref, tri_ref)


def _token_mixer(x, nw, win, gb, pw, ps, pup, lbf, oml, dl, hnw, hup, wout, router=None):
    b, s, d = x.shape
    tm = MIXER_TM
    pw_cols = pup.shape[0]
    hw = hup.shape[0]
    hd = hw // HGRN_HEADS
    assert s % tm == 0 and tm >= POOL_CARRY
    ns = s // tm
    consts = (nw, win, gb, pw, ps, pup, lbf, oml, dl, hnw, hup, wout)
    tile = lambda width: pl.BlockSpec((1, tm, width), lambda bi, si: (bi, si, 0))
    out_specs = [tile(d)]
    out_shape = [jax.ShapeDtypeStruct(x.shape, F32)]
    if router is not None:
        assert tm == MOE_TC and router[1].shape[0] <= LANE
        ne = router[1].shape[0]
        consts += (router[0], router[1], jnp.tril(jnp.ones((tm, tm), BF16)))
        out_specs += [tile(d), tile(TOP_K), tile(TOP_K), tile(TOP_K),
                      pl.BlockSpec((1, 1, 1, LANE), lambda bi, si: (bi, si, 0, 0))]
        out_shape += [jax.ShapeDtypeStruct((b, s, d), BF16),
                      jax.ShapeDtypeStruct((b, s, TOP_K), jnp.int32),
                      jax.ShapeDtypeStruct((b, s, TOP_K), F32),
                      jax.ShapeDtypeStruct((b, s, TOP_K), jnp.int32),
                      jax.ShapeDtypeStruct((b, ns, 1, LANE), jnp.int32)]
    outs = pl.pallas_call(
        functools.partial(_mixer_kernel, route=router is not None),
        grid=(b, ns),
        in_specs=[tile(d)] + [_const_spec(a.shape) for a in consts],
        out_specs=out_specs,
        out_shape=out_shape,
        scratch_shapes=[pltpu.VMEM((POOL_CARRY, pw_cols), F32),
                        pltpu.VMEM((HGRN_HEADS, hd, hd), F32)],
        compiler_params=pltpu.CompilerParams(
            dimension_semantics=("arbitrary", "arbitrary"),
            vmem_limit_bytes=VMEM_LIMIT),
        name="token_mixer",
    )(x, *consts)
    if router is None:
        return outs[0]
    t = b * s
    x_new, h, top_i, top_w, rank, cnt = outs
    return (x_new, h.reshape(t, d), top_i.reshape(t, TOP_K), top_w.reshape(t, TOP_K),
            rank.reshape(t, TOP_K), cnt.reshape(b * ns, LANE)[:, :ne])


def _dense_ffn_kernel(x_ref, nw_ref, wg_ref, wu_ref, wd_ref, *rest, n_chunks, n_cast):
    cast_in, o_ref = rest[:n_cast], rest[n_cast]
    cast_out, zero_out = rest[n_cast + 1:2 * n_cast + 1], rest[2 * n_cast + 1:]
    x = x_ref[...]
    h = _rmsnorm(x, nw_ref[...]).astype(BF16)
    ff = wg_ref.shape[1]
    fc = ff // n_chunks
    out = x
    for ci in range(n_chunks):
        sl = slice(ci * fc, (ci + 1) * fc)
        gt = _dot(h, wg_ref[:, sl])
        up = _dot(h, wu_ref[:, sl])
        a = (gt * _sigmoid(gt) * up).astype(BF16)
        out = out + _dot(a, wd_ref[sl, :])
    o_ref[...] = out
    for src, dst in zip(cast_in, cast_out):
        dst[...] = src[...].astype(dst.dtype)
    for z in zero_out:
        z[...] = jnp.zeros_like(z)


def _dense_ffn(x2, nw, wg, wu, wd, to_cast=(), zero_rows=0):
    t, d = x2.shape
    tm = DENSE_TM
    ff = wg.shape[1]
    n_chunks = 2 if ff % (2 * LANE) == 0 else 1
    assert t % tm == 0
    n = t // tm
    slabs = []
    for a in to_cast:
        rows = a.size // a.shape[-1]
        assert rows % (n * BF16_SUBLANES) == 0
        slabs.append(a.reshape(n, rows // n, a.shape[-1]))
    slab_specs = [pl.BlockSpec((1,) + s.shape[1:], lambda i: (i, 0, 0)) for s in slabs]
    zero_specs, zero_shapes = [], []
    if zero_rows:
        assert zero_rows % (n * BF16_SUBLANES) == 0
        zero_specs = [pl.BlockSpec((zero_rows // n, d), lambda i: (i, 0))]
        zero_shapes = [jax.ShapeDtypeStruct((zero_rows, d), BF16)]
    outs = pl.pallas_call(
        functools.partial(_dense_ffn_kernel, n_chunks=n_chunks, n_cast=len(slabs)),
        grid=(n,),
        in_specs=[pl.BlockSpec((tm, d), lambda i: (i, 0)),
                  _const_spec(nw.shape), _const_spec(wg.shape),
                  _const_spec(wu.shape), _const_spec(wd.shape)] + slab_specs,
        out_specs=[pl.BlockSpec((tm, d), lambda i: (i, 0))] + slab_specs + zero_specs,
        out_shape=[jax.ShapeDtypeStruct(x2.shape, F32)]
        + [jax.ShapeDtypeStruct(s.shape, BF16) for s in slabs] + zero_shapes,
        compiler_params=pltpu.CompilerParams(
            dimension_semantics=("arbitrary",), vmem_limit_bytes=VMEM_LIMIT),
        name="dense_ffn",
    )(x2, nw, wg, wu, wd, *slabs)
    casted = [o.reshape(a.shape) for o, a in zip(outs[1:1 + len(slabs)], to_cast)]
    return outs[0], casted, (outs[-1] if zero_rows else None)


def _route_tile(x, nw_ref, r_ref, tri_ref):
    h = _rmsnorm(x, nw_ref[...])
    ne = r_ref.shape[0]
    logit = [jnp.sum(h * r_ref[e:e + 1, :], axis=-1, keepdims=True) for e in range(ne)]
    m1 = logit[0]
    i1 = jnp.zeros(m1.shape, jnp.int32)
    for e in range(1, ne):
        better = logit[e] > m1
        m1 = jnp.where(better, logit[e], m1)
        i1 = jnp.where(better, e, i1)
    m2 = jnp.full(m1.shape, -jnp.inf, F32)
    i2 = jnp.zeros(m1.shape, jnp.int32)
    for e in range(ne):
        cand = jnp.where(i1 == e, -jnp.inf, logit[e])
        better = cand > m2
        m2 = jnp.where(better, cand, m2)
        i2 = jnp.where(better, e, i2)
    e2 = jnp.exp(m2 - m1)
    den = 1.0 + e2
    slot = lax.broadcasted_iota(jnp.int32, (x.shape[0], TOP_K), 1)
    idx = jnp.where(slot == 0, i1, i2)
    wts = jnp.where(slot == 0, 1.0 / den, e2 / den)
    lanes = lax.broadcasted_iota(jnp.int32, (1, LANE), 1)
    chosen = jnp.where(jnp.logical_or(lanes == i1, lanes == i2), 1.0, 0.0).astype(BF16)
    csum = _dot(tri_ref[...], chosen)
    rank1 = jnp.sum(jnp.where(lanes == i1, csum, 0.0), axis=-1, keepdims=True)
    rank2 = jnp.sum(jnp.where(lanes == i2, csum, 0.0), axis=-1, keepdims=True)
    rank = (jnp.where(slot == 0, rank1, rank2) - 1.0).astype(jnp.int32)
    tm = x.shape[0]
    return h.astype(BF16), idx, wts, rank, csum[tm - 1:tm, :].astype(jnp.int32)


def _route(tile_cnt, bm, tc):
    nt, n_experts = tile_cnt.shape
    counts = jnp.sum(tile_cnt, axis=0)
    nb_e = (counts + bm - 1) // bm + 1
    blk_end = jnp.cumsum(nb_e)
    blk_start = blk_end - nb_e
    row0 = blk_start * bm

    nb = (nt * tc * TOP_K) // bm + 2 * n_experts
    bi = jnp.arange(nb, dtype=jnp.int32)
    blk_e = jnp.minimum(jnp.sum((bi[:, None] >= blk_end[None, :]).astype(jnp.int32), axis=1),
                        n_experts - 1)
    blk_valid = jnp.clip(counts[blk_e] - (bi - blk_start[blk_e]) * bm, 0, bm)
    blk_valid = jnp.where(bi < blk_end[-1], blk_valid, 0)

    tile_prev = jnp.cumsum(tile_cnt, axis=0) - tile_cnt
    pair_row = row0[None, :] + tile_prev
    i32 = lambda a: a.astype(jnp.int32)
    return i32(blk_e), i32(blk_valid), i32(pair_row), i32(tile_cnt)


def _dispatch_window_write(wbuf, x_hbm, sem, start, slot, e):
    return pltpu.make_async_copy(wbuf.at[slot, e], x_hbm.at[pl.ds(start, DISPATCH_WIN), :],
                                 sem.at[slot, e])


def _dispatch_kernel(row_ref, cnt_ref, h_ref, idx_ref, rank_ref, zeros_hbm, x_hbm,
                     wbuf, xbuf, carry, sem, xsem):
    del zeros_hbm
    j = pl.program_id(0)
    nt = pl.num_programs(0)
    ne = carry.shape[0]
    win = DISPATCH_WIN
    grp = BF16_SUBLANES
    slot = j % 2

    @pl.when(j == 0)
    def _():
        carry[...] = jnp.zeros_like(carry)

    @pl.when(j > 0)
    def _():
        for e in range(ne):
            _dispatch_window_write(wbuf, x_hbm, sem, 0, 1 - slot, e).wait()

    h = h_ref[...]
    idx = idx_ref[0]
    rank = rank_ref[0]
    rows = lax.broadcasted_iota(jnp.int32, (win, 1), 0)

    def selection(e, lead):
        hit = None
        for kk in range(TOP_K):
            off = jnp.where(idx[kk:kk + 1, :] == e, rank[kk:kk + 1, :] + lead, -1)
            hk = off == rows
            hit = hk if hit is None else jnp.logical_or(hit, hk)
        return jnp.where(hit, 1.0, 0.0).astype(BF16)

    a0s, fills, n_mores = [], [], []
    for e in range(ne):
        row = row_ref[j * ne + e]
        cnt = cnt_ref[j * ne + e]
        a0 = pl.multiple_of((row // grp) * grp, grp)
        fill = row - a0 + cnt
        w = _dot(selection(e, row - a0), h)
        head = w[0:grp, :] + carry[e].astype(F32)
        wbuf[slot, e, 0:grp, :] = head.astype(BF16)
        wbuf[slot, e, grp:, :] = w[grp:, :].astype(BF16)
        _dispatch_window_write(wbuf, x_hbm, sem, a0, slot, e).start()
        g = pl.multiple_of(jnp.minimum((fill // grp) * grp, win - grp), grp)
        carry[e] = wbuf[slot, e, pl.ds(g, grp), :]
        a0s.append(a0)
        fills.append(fill)
        n_mores.append(fill // win)

    @pl.when(functools.reduce(lambda a, b: a + b, n_mores) > 0)
    def _():
        for e in range(ne):
            a0, fill, n_more = a0s[e], fills[e], n_mores[e]

            @pl.when(n_more > 0)
            def _(e=e, a0=a0, fill=fill, n_more=n_more):
                lead0 = row_ref[j * ne + e] - a0

                def more(m, c):
                    st = pl.multiple_of(a0 + (m + 1) * win, grp)
                    xbuf[...] = _dot(selection(e, lead0 - (m + 1) * win), h).astype(BF16)
                    cp = pltpu.make_async_copy(xbuf, x_hbm.at[pl.ds(st, win), :], xsem.at[0])
                    cp.start()
                    cp.wait()
                    return c

                lax.fori_loop(0, n_more, more, 0)
                g = pl.multiple_of(((fill - n_more * win) // grp) * grp, grp)
                carry[e] = xbuf[pl.ds(g, grp), :]

    @pl.when(j == nt - 1)
    def _():
        for e in range(ne):
            _dispatch_window_write(wbuf, x_hbm, sem, 0, slot, e).wait()


def _moe_dispatch(h, top_i, rank, pair_row, pair_cnt, zero_rows, tc):
    t, d = h.shape
    n_rows = zero_rows.shape[0]
    nt = t // tc
    ne = pair_row.shape[1]
    lane_major = lambda a: a.reshape(nt, tc, TOP_K).transpose(0, 2, 1)
    grid_spec = pltpu.PrefetchScalarGridSpec(
        num_scalar_prefetch=2,
        grid=(nt,),
        in_specs=[pl.BlockSpec((tc, d), lambda j, a, b: (j, 0)),
                  pl.BlockSpec((1, TOP_K, tc), lambda j, a, b: (j, 0, 0)),
                  pl.BlockSpec((1, TOP_K, tc), lambda j, a, b: (j, 0, 0)),
                  pl.BlockSpec(memory_space=pl.ANY)],
        out_specs=pl.BlockSpec(memory_space=pl.ANY),
        scratch_shapes=[pltpu.VMEM((2, ne, DISPATCH_WIN, d), BF16),
                        pltpu.VMEM((DISPATCH_WIN, d), BF16),
                        pltpu.VMEM((ne, BF16_SUBLANES, d), BF16),
                        pltpu.SemaphoreType.DMA((2, ne)),
                        pltpu.SemaphoreType.DMA((1,))],
    )
    return pl.pallas_call(
        _dispatch_kernel,
        grid_spec=grid_spec,
        out_shape=jax.ShapeDtypeStruct((n_rows, d), BF16),
        input_output_aliases={5: 0},
        compiler_params=pltpu.CompilerParams(
            dimension_semantics=("arbitrary",), vmem_limit_bytes=VMEM_LIMIT),
        name="moe_dispatch",
    )(pair_row.reshape(-1), pair_cnt.reshape(-1), h, lane_major(top_i), lane_major(rank),
      zero_rows)


def _experts_kernel(blk_e_ref, valid_ref, x_ref, wg_ref, wu_ref, wd_ref, o_ref, acc):
    del blk_e_ref
    i = pl.program_id(0)
    f = pl.program_id(1)
    nf = pl.num_programs(1)
    active = valid_ref[i] > 0
    first = f == 0
    final = f == nf - 1

    def partial_out():
        xb = x_ref[...]
        gt = _dot(xb, wg_ref[0])
        up = _dot(xb, wu_ref[0])
        a = (gt * _sigmoid(gt) * up).astype(BF16)
        return _dot(a, wd_ref[0])

    @pl.when(active & first & final)
    def _():
        o_ref[...] = partial_out().astype(o_ref.dtype)

    @pl.when(active & first & jnp.logical_not(final))
    def _():
        acc[...] = partial_out()

    @pl.when(active & jnp.logical_not(first) & jnp.logical_not(final))
    def _():
        acc[...] += partial_out()

    @pl.when(active & jnp.logical_not(first) & final)
    def _():
        o_ref[...] = (acc[...] + partial_out()).astype(o_ref.dtype)

    @pl.when(jnp.logical_not(active) & final)
    def _():
        o_ref[...] = jnp.zeros_like(o_ref)


def _moe_experts(x_sorted, blk_e, blk_valid, wg, wu, wd, bm):
    rows, d = x_sorted.shape
    nb = rows // bm
    tf = MOE_TF
    ff = wg.shape[2]
    assert ff % tf == 0
    nf = ff // tf

    def wcol(i, f, bv):
        return jnp.where(bv[i] > 0, f, nf - 1)

    grid_spec = pltpu.PrefetchScalarGridSpec(
        num_scalar_prefetch=2,
        grid=(nb, nf),
        in_specs=[
            pl.BlockSpec((bm, d), lambda i, f, be, bv: (i, 0)),
            pl.BlockSpec((1, d, tf), lambda i, f, be, bv: (be[i], 0, wcol(i, f, bv))),
            pl.BlockSpec((1, d, tf), lambda i, f, be, bv: (be[i], 0, wcol(i, f, bv))),
            pl.BlockSpec((1, tf, d), lambda i, f, be, bv: (be[i], wcol(i, f, bv), 0)),
        ],
        out_specs=pl.BlockSpec((bm, d), lambda i, f, be, bv: (i, 0)),
        scratch_shapes=[pltpu.VMEM((bm, d), F32)],
    )
    return pl.pallas_call(
        _experts_kernel,
        grid_spec=grid_spec,
        out_shape=jax.ShapeDtypeStruct((rows, d), BF16),
        compiler_params=pltpu.CompilerParams(
            dimension_semantics=("arbitrary", "arbitrary"),
            vmem_limit_bytes=VMEM_LIMIT),
        name="moe_experts",
    )(blk_e, blk_valid, x_sorted, wg, wu, wd)


def _window_start(row, n_rows):
    a0 = (row // BF16_SUBLANES) * BF16_SUBLANES
    return pl.multiple_of(jnp.minimum(a0, n_rows - COMBINE_WIN), BF16_SUBLANES)


def _window_copy(y_hbm, ybuf, sem, start, slot, e):
    return pltpu.make_async_copy(y_hbm.at[pl.ds(start, COMBINE_WIN), :],
                                 ybuf.at[slot, pl.ds(e * COMBINE_WIN, COMBINE_WIN), :],
                                 sem.at[slot, e])


def _combine_kernel(row_ref, cnt_ref, x_ref, idx_ref, w_ref, rank_ref, nw_ref, y_hbm,
                    o_ref, ybuf, xtra, zacc, sem, xsem, *, apply_norm):
    j = pl.program_id(0)
    nt = pl.num_programs(0)
    ne = ybuf.shape[1] // COMBINE_WIN
    n_rows = y_hbm.shape[0]
    slot = j % 2

    def start_tile(tile, s):
        for e in range(ne):
            _window_copy(y_hbm, ybuf, sem, _window_start(row_ref[tile * ne + e], n_rows), s, e).start()

    @pl.when(j == 0)
    def _():
        start_tile(0, 0)

    @pl.when(j + 1 < nt)
    def _():
        start_tile(j + 1, 1 - slot)

    idx = idx_ref[...]
    wts = w_ref[...]
    rank = rank_ref[...]
    lanes = lax.broadcasted_iota(jnp.int32, (1, COMBINE_WIN), 1)

    def weights_tile(e, start, skip):
        m = jnp.zeros((idx.shape[0], COMBINE_WIN), F32)
        lead = row_ref[j * ne + e] - start
        for kk in range(TOP_K):
            off = rank[:, kk:kk + 1] + lead
            valid = jnp.logical_and(idx[:, kk:kk + 1] == e, off >= skip)
            off = jnp.where(valid, off, -1)
            m = jnp.where(off == lanes, wts[:, kk:kk + 1], m)
        return m.astype(BF16)

    def finish(s):
        starts = [_window_start(row_ref[j * ne + e], n_rows) for e in range(ne)]
        for e in range(ne):
            _window_copy(y_hbm, ybuf, sem, starts[e], s, e).wait()
        z = x_ref[...]
        for e in range(ne):
            z = z + _dot(weights_tile(e, starts[e], 0),
                         ybuf[s, e * COMBINE_WIN:(e + 1) * COMBINE_WIN, :])
        n_more = [jnp.maximum((row_ref[j * ne + e] - starts[e] + cnt_ref[j * ne + e]
                               + COMBINE_WIN - 1) // COMBINE_WIN - 1, 0) for e in range(ne)]
        any_more = functools.reduce(lambda a, b: a + b, n_more)

        def emit(zz):
            o_ref[...] = _rmsnorm(zz, nw_ref[...]) if apply_norm else zz

        zacc[...] = z
        emit(z)

        @pl.when(any_more > 0)
        def _():
            for e in range(ne):
                def more(m, carry, e=e, start=starts[e]):
                    want = start + (m + 1) * COMBINE_WIN
                    st = pl.multiple_of(jnp.minimum(want, n_rows - COMBINE_WIN), BF16_SUBLANES)
                    cp = pltpu.make_async_copy(y_hbm.at[pl.ds(st, COMBINE_WIN), :], xtra, xsem.at[0])
                    cp.start()
                    cp.wait()
                    zacc[...] += _dot(weights_tile(e, st, want - st), xtra[...])
                    return carry

                lax.fori_loop(0, n_more[e], more, 0)
            emit(zacc[...])

    finish(slot)


def _moe_combine(x2, y_sorted, top_i, top_w, rank, pair_row, pair_cnt, nw, apply_norm):
    t, d = x2.shape
    tm = MOE_TC
    nt = t // tm
    ne = pair_row.shape[1]
    assert y_sorted.shape[0] >= COMBINE_WIN
    grid_spec = pltpu.PrefetchScalarGridSpec(
        num_scalar_prefetch=2,
        grid=(nt,),
        in_specs=[
            pl.BlockSpec((tm, d), lambda i, a, b: (i, 0)),
            pl.BlockSpec((tm, TOP_K), lambda i, a, b: (i, 0)),
            pl.BlockSpec((tm, TOP_K), lambda i, a, b: (i, 0)),
            pl.BlockSpec((tm, TOP_K), lambda i, a, b: (i, 0)),
            pl.BlockSpec(nw.shape, lambda i, a, b: (0, 0)),
            pl.BlockSpec(memory_space=pl.ANY),
        ],
        out_specs=pl.BlockSpec((tm, d), lambda i, a, b: (i, 0)),
        scratch_shapes=[pltpu.VMEM((2, ne * COMBINE_WIN, d), BF16),
                        pltpu.VMEM((COMBINE_WIN, d), BF16),
                        pltpu.VMEM((tm, d), F32),
                        pltpu.SemaphoreType.DMA((2, ne)),
                        pltpu.SemaphoreType.DMA((1,))],
    )
    return pl.pallas_call(
        functools.partial(_combine_kernel, apply_norm=apply_norm),
        grid_spec=grid_spec,
        out_shape=jax.ShapeDtypeStruct((t, d), F32),
        compiler_params=pltpu.CompilerParams(
            dimension_semantics=("arbitrary",), vmem_limit_bytes=VMEM_LIMIT),
        name="moe_combine",
    )(pair_row.reshape(-1), pair_cnt.reshape(-1), x2, top_i, top_w, rank, nw, y_sorted)


def _final_norm_kernel(x_ref, nw_ref, o_ref):
    o_ref[...] = _rmsnorm(x_ref[...], nw_ref[...])


def _final_norm(x2, nw):
    t, d = x2.shape
    tm = DENSE_TM
    return pl.pallas_call(
        _final_norm_kernel,
        grid=(t // tm,),
        in_specs=[pl.BlockSpec((tm, d), lambda i: (i, 0)), _const_spec(nw.shape)],
        out_specs=pl.BlockSpec((tm, d), lambda i: (i, 0)),
        out_shape=jax.ShapeDtypeStruct((t, d), F32),
        name="final_norm",
    )(x2, nw)


def kernel(x, attn_norm_w, w_in, gate_bias, pool_w, pool_scale, pool_up, hgrn_lower_bounds,
           hgrn_norm_w, hgrn_up, w_out, ffn_norm_w, dense_w_gate, dense_w_up, dense_w_down,
           moe_router, moe_w_gate, moe_w_up, moe_w_down, final_norm_w):
    b, s, d = x.shape
    depth = w_in.shape[0]
    n_experts = moe_router.shape[-1]

    lb_soft = jax.nn.softmax(hgrn_lower_bounds.astype(F32), axis=0)
    lb = jnp.clip(jnp.cumsum(lb_soft, axis=0) - lb_soft[0:1], 0.0, 1.0 - 1e-6)
    lbf = jnp.maximum(lb, LB_FLOOR)
    oml = 1.0 - lb
    dlb = lb - lbf

    def row(a):
        return a.reshape(1, -1).astype(F32)

    final_w = row(final_norm_w)
    n_sorted_rows = (b * s * TOP_K // MOE_BM + 2 * n_experts) * MOE_BM
    normed = None
    for l in range(depth):
        j = l // 2
        last = l == depth - 1
        routed = l % 2 == 1
        mixed = _token_mixer(
            x, row(attn_norm_w[l]), w_in[l].astype(BF16), row(gate_bias[l]),
            pool_w[l].astype(BF16), row(pool_scale[l]), pool_up[l].astype(BF16),
            row(lbf[l]), row(oml[l]), row(dlb[l]),
            row(jnp.tile(hgrn_norm_w[l], HGRN_HEADS)), hgrn_up[l].astype(BF16),
            w_out[l].astype(BF16),
            router=(row(ffn_norm_w[l]), moe_router[j].astype(F32).T) if routed else None)
        if routed:
            x, h, top_i, top_w, rank, tile_cnt = mixed
        else:
            x = mixed
        x2 = x.reshape(b * s, d)
        if l % 2 == 0:
            routed_next = l + 1 < depth
            to_cast = (moe_w_gate[j], moe_w_up[j], moe_w_down[j]) if routed_next else ()
            x2, moe_w16, sorted_zeros = _dense_ffn(
                x2, row(ffn_norm_w[l]), dense_w_gate[j].astype(BF16),
                dense_w_up[j].astype(BF16), dense_w_down[j].astype(BF16),
                to_cast, n_sorted_rows if routed_next else 0)
            if last:
                normed = _final_norm(x2, final_w)
        else:
            blk_e, blk_valid, pair_row, pair_cnt = _route(tile_cnt, MOE_BM, MOE_TC)
            x_sorted = _moe_dispatch(h, top_i, rank, pair_row, pair_cnt, sorted_zeros, MOE_TC)
            y_sorted = _moe_experts(x_sorted, blk_e, blk_valid, *moe_w16, MOE_BM)
            x2 = _moe_combine(x2, y_sorted, top_i, top_w, rank, pair_row, pair_cnt, final_w,
                              apply_norm=last)
            if last:
                normed = x2
        x = x2.reshape(b, s, d)
    return normed.reshape(b, s, d)
```

```python
import functools

import jax
import jax.numpy as jnp
from jax import lax
from jax.experimental import pallas as pl
from jax.experimental.pallas import tpu as pltpu

F32 = jnp.float32
BF16 = jnp.bfloat16

EPS = 1e-6
LB_FLOOR = 1e-30
POOL_WINDOWS = (2, 4, 8, 16)
N_POOL_GROUPS = 4
HGRN_HEADS = 4
TOP_K = 2
LANE = 128
BF16_SUBLANES = 16

MIXER_TM = 512
HGRN_CHUNK = 256
POOL_CARRY = 16
DENSE_TM = 512
MOE_BM = 512
MOE_TF = 1792
WEIGHT_BUFFERS = 3
MOE_TC = 512
DISPATCH_WIN = 192
COMBINE_WIN = 256
VMEM_LIMIT = 56 * 1024 * 1024


def _const_spec(shape):
    nd = len(shape)
    return pl.BlockSpec(shape, lambda *_: (0,) * nd, pipeline_mode=pl.Buffered(1))


def _dot(a, b):
    return jnp.dot(a, b, preferred_element_type=F32)


def _dot_nt(a, b):
    return lax.dot_general(a, b, (((1,), (1,)), ((), ())), preferred_element_type=F32)


def _dot_tn(a, b):
    return lax.dot_general(a, b, (((0,), (0,)), ((), ())), preferred_element_type=F32)


def _sigmoid(x):
    return 0.5 * jnp.tanh(0.5 * x) + 0.5


def _rmsnorm(x, w):
    ms = jnp.mean(x * x, axis=-1, keepdims=True)
    return x * lax.rsqrt(ms + EPS) * w


def _hgrn_chunk(q, k, v, c, state_ref, hd, side_work):
    n = q.shape[0]
    rows = lax.broadcasted_iota(jnp.int32, (n, 1), 0)
    cols = lax.broadcasted_iota(jnp.int32, (1, n), 1)
    v16 = v.astype(BF16)
    scores = [jnp.zeros((n, n), F32) for _ in range(HGRN_HEADS)]

    p = c
    tot = c
    half = 1
    shift = 1
    while half < n:
        upper = (rows & half) != 0
        e = jnp.exp2(jnp.where(upper, p, tot - p))
        row_key = jnp.where(upper, rows >> shift, -1)
        col_key = jnp.where((cols & half) != 0, -2, cols >> shift)
        owned = row_key == col_key
        for hi in range(HGRN_HEADS):
            sl = slice(hi * hd, (hi + 1) * hd)
            eh = e[:, sl]
            qt = (q[:, sl] * eh).astype(BF16)
            kt = (k[:, sl] * eh).astype(BF16)
            scores[hi] = jnp.where(owned, _dot_nt(qt, kt), scores[hi])
        other = jnp.where(upper, pltpu.roll(tot, half, 0), pltpu.roll(tot, n - half, 0))
        p = p + jnp.where(upper, other, 0.0)
        tot = tot + other
        if side_work:
            if side_work[0][0] <= 0:
                side_work.pop(0)[1]()
            else:
                side_work[0][0] -= 1
        half *= 2
        shift += 1

    eg = jnp.exp2(p)
    er = jnp.exp2(tot - p)
    o_heads = []
    for hi in range(HGRN_HEADS):
        sl = slice(hi * hd, (hi + 1) * hd)
        qh, kh, vh = q[:, sl], k[:, sl], v[:, sl]
        st = state_ref[hi]
        o = _dot(scores[hi].astype(BF16), v16[:, sl])
        o = o + jnp.sum(qh * kh, axis=-1, keepdims=True) * vh
        o = o + _dot_nt((qh * eg[:, sl]).astype(BF16), st.astype(BF16))
        kd = (kh * er[:, sl]).astype(BF16)
        state_ref[hi] = st * jnp.exp2(tot[0:1, sl]) + _dot_tn(v16[:, sl], kd)
        o = o * lax.rsqrt(jnp.mean(o * o, axis=-1, keepdims=True) + EPS)
        o_heads.append(o)
    return jnp.concatenate(o_heads, axis=1)


def _mixer_kernel(x_ref, nw_ref, win_ref, gb_ref, pw_ref, ps_ref, pup_ref,
                  lbf_ref, oml_ref, dl_ref, hnw_ref, hup_ref, wout_ref, *rest, route):
    if route:
        (fnw_ref, r_ref, tri_ref, o_ref, h_ref, idx_ref, w_ref, rank_ref, cnt_ref,
         carry_ref, state_ref) = rest
    else:
        o_ref, carry_ref, state_ref = rest
    tm = x_ref.shape[1]
    d = x_ref.shape[2]
    pw_cols = pup_ref.shape[0]
    hw = hup_ref.shape[0]
    gd = pw_cols // N_POOL_GROUPS
    hd = hw // HGRN_HEADS
    off_q = pw_cols
    off_f = off_q + hw
    off_i = off_f + hw
    off_g = off_i + hw
    off_gates = off_g + hw
    si = pl.program_id(1)

    @pl.when(si == 0)
    def _():
        carry_ref[...] = jnp.zeros_like(carry_ref)
        state_ref[...] = jnp.zeros_like(state_ref)

    x = x_ref[0]
    h = _rmsnorm(x, nw_ref[...]).astype(BF16)

    def pool_branch():
        u = _dot(h, win_ref[:, 0:pw_cols])
        ext = jnp.concatenate([carry_ref[...], u], axis=0)
        carry_ref[...] = u[tm - POOL_CARRY:, :]
        pos = (si * tm + 1 + lax.broadcasted_iota(jnp.int32, (tm, 1), 0)).astype(F32)
        acc = ext
        span = 1
        mixed = []
        for gi, w in enumerate(POOL_WINDOWS):
            while span < w:
                acc = acc + pltpu.roll(acc, span, 0)
                span *= 2
            wsum = acc[POOL_CARRY:, 0:gd]
            acc = acc[:, gd:] if gi + 1 < N_POOL_GROUPS else None
            ug = u[:, gi * gd:(gi + 1) * gd]
            pooled = wsum * (1.0 / jnp.minimum(pos, float(w))) - ug
            mixed.append(_dot(pooled.astype(BF16), pw_ref[gi]))
        mixed = jnp.concatenate(mixed, axis=1) * ps_ref[...]
        return _dot(mixed.astype(BF16), pup_ref[...])

    q_pre = _dot(h, win_ref[:, off_q:off_f])
    q = q_pre * _sigmoid(q_pre)
    fp = _dot(h, win_ref[:, off_f:off_i])
    v = _dot(h, win_ref[:, off_i:off_g])
    g_pre = _dot(h, win_ref[:, off_g:off_gates])
    g = g_pre * _sigmoid(g_pre)

    en = jnp.exp(-jnp.abs(fp))
    rden = 1.0 / (1.0 + en)
    nonneg = fp >= 0
    spos = jnp.where(nonneg, 1.0, en) * rden
    sneg = jnp.where(nonneg, en, 1.0) * rden
    c = jnp.log2(lbf_ref[...] + oml_ref[...] * spos)
    k = oml_ref[...] * sneg + dl_ref[...]

    gz_parts = []
    y_pool_out = []
    gate_cols = 2 * LANE
    side_work = [
        [0, functools.partial(
            lambda c0: gz_parts.append(_dot(h, win_ref[:, off_gates + c0:off_gates + c0 + gate_cols])),
            c0)]
        for c0 in range(0, 2 * d, gate_cols)]
    side_work.insert(len(side_work) // 2, [0, lambda: y_pool_out.append(pool_branch())])
    ck = HGRN_CHUNK
    o_chunks = [_hgrn_chunk(q[r0:r0 + ck], k[r0:r0 + ck], v[r0:r0 + ck], c[r0:r0 + ck],
                            state_ref, hd, side_work)
                for r0 in range(0, tm, ck)]
    while side_work:
        side_work.pop(0)[1]()
    o_all = jnp.concatenate(o_chunks, axis=0) * hnw_ref[...] * g
    y_hgrn = _dot(o_all.astype(BF16), hup_ref[...])

    gz = jnp.concatenate(gz_parts, axis=1) + gb_ref[...]
    gates = _sigmoid(gz)
    merged = gates[:, 0:d] * y_pool_out[0] + gates[:, d:2 * d] * y_hgrn
    x_new = x + _dot(merged.astype(BF16), wout_ref[...])
    o_ref[0] = x_new
    if route:
        h_ref[0], idx_ref[0], w_ref[0], rank_ref[0], cnt_ref[0, 0] = _route_tile(
            x_new, fnw_ref, r_ref, tri_ref)


def _token_mixer(x, nw, win, gb, pw, ps, pup, lbf, oml, dl, hnw, hup, wout, router=None):
    b, s, d = x.shape
    tm = MIXER_TM
    pw_cols = pup.shape[0]
    hw = hup.shape[0]
    hd = hw // HGRN_HEADS
    assert s % tm == 0 and tm >= POOL_CARRY
    ns = s // tm
    consts = (nw, win, gb, pw, ps, pup, lbf, oml, dl, hnw, hup, wout)
    tile = lambda width: pl.BlockSpec((1, tm, width), lambda bi, si: (bi, si, 0))
    out_specs = [tile(d)]
    out_shape = [jax.ShapeDtypeStruct(x.shape, F32)]
    if router is not None:
        assert tm == MOE_TC and router[1].shape[0] <= LANE
        ne = router[1].shape[0]
        consts += (router[0], router[1], jnp.tril(jnp.ones((tm, tm), BF16)))
        out_specs += [tile(d), tile(TOP_K), tile(TOP_K), tile(TOP_K),
                      pl.BlockSpec((1, 1, 1, LANE), lambda bi, si: (bi, si, 0, 0))]
        out_shape += [jax.ShapeDtypeStruct((b, s, d), BF16),
                      jax.ShapeDtypeStruct((b, s, TOP_K), jnp.int32),
                      jax.ShapeDtypeStruct((b, s, TOP_K), F32),
                      jax.ShapeDtypeStruct((b, s, TOP_K), jnp.int32),
                      jax.ShapeDtypeStruct((b, ns, 1, LANE), jnp.int32)]
    outs = pl.pallas_call(
        functools.partial(_mixer_kernel, route=router is not None),
        grid=(b, ns),
        in_specs=[tile(d)] + [_const_spec(a.shape) for a in consts],
        out_specs=out_specs,
        out_shape=out_shape,
        scratch_shapes=[pltpu.VMEM((POOL_CARRY, pw_cols), F32),
                        pltpu.VMEM((HGRN_HEADS, hd, hd), F32)],
        compiler_params=pltpu.CompilerParams(
            dimension_semantics=("arbitrary", "arbitrary"),
            vmem_limit_bytes=VMEM_LIMIT),
        name="token_mixer",
    )(x, *consts)
    if router is None:
        return outs[0]
    t = b * s
    x_new, h, top_i, top_w, rank, cnt = outs
    return (x_new, h.reshape(t, d), top_i.reshape(t, TOP_K), top_w.reshape(t, TOP_K),
            rank.reshape(t, TOP_K), cnt.reshape(b * ns, LANE)[:, :ne])


def _dense_ffn_kernel(x_ref, nw_ref, wg_ref, wu_ref, wd_ref, *rest, n_chunks, n_cast):
    cast_in, o_ref = rest[:n_cast], rest[n_cast]
    cast_out, zero_out = rest[n_cast + 1:2 * n_cast + 1], rest[2 * n_cast + 1:]
    x = x_ref[...]
    h = _rmsnorm(x, nw_ref[...]).astype(BF16)
    ff = wg_ref.shape[1]
    fc = ff // n_chunks
    out = x
    for ci in range(n_chunks):
        sl = slice(ci * fc, (ci + 1) * fc)
        gt = _dot(h, wg_ref[:, sl])
        up = _dot(h, wu_ref[:, sl])
        a = (gt * _sigmoid(gt) * up).astype(BF16)
        out = out + _dot(a, wd_ref[sl, :])
    o_ref[...] = out
    for src, dst in zip(cast_in, cast_out):
        dst[...] = src[...].astype(dst.dtype)
    for z in zero_out:
        z[...] = jnp.zeros_like(z)


def _dense_ffn(x2, nw, wg, wu, wd, to_cast=(), zero_rows=0):
    t, d = x2.shape
    tm = DENSE_TM
    ff = wg.shape[1]
    n_chunks = 2 if ff % (2 * LANE) == 0 else 1
    assert t % tm == 0
    n = t // tm
    slabs = []
    for a in to_cast:
        rows = a.size // a.shape[-1]
        assert rows % (n * BF16_SUBLANES) == 0
        slabs.append(a.reshape(n, rows // n, a.shape[-1]))
    slab_specs = [pl.BlockSpec((1,) + s.shape[1:], lambda i: (i, 0, 0)) for s in slabs]
    zero_specs, zero_shapes = [], []
    if zero_rows:
        assert zero_rows % (n * BF16_SUBLANES) == 0
        zero_specs = [pl.BlockSpec((zero_rows // n, d), lambda i: (i, 0))]
        zero_shapes = [jax.ShapeDtypeStruct((zero_rows, d), BF16)]
    outs = pl.pallas_call(
        functools.partial(_dense_ffn_kernel, n_chunks=n_chunks, n_cast=len(slabs)),
        grid=(n,),
        in_specs=[pl.BlockSpec((tm, d), lambda i: (i, 0)),
                  _const_spec(nw.shape), _const_spec(wg.shape),
                  _const_spec(wu.shape), _const_spec(wd.shape)] + slab_specs,
        out_specs=[pl.BlockSpec((tm, d), lambda i: (i, 0))] + slab_specs + zero_specs,
        out_shape=[jax.ShapeDtypeStruct(x2.shape, F32)]
        + [jax.ShapeDtypeStruct(s.shape, BF16) for s in slabs] + zero_shapes,
        compiler_params=pltpu.CompilerParams(
            dimension_semantics=("arbitrary",), vmem_limit_bytes=VMEM_LIMIT),
        name="dense_ffn",
    )(x2, nw, wg, wu, wd, *slabs)
    casted = [o.reshape(a.shape) for o, a in zip(outs[1:1 + len(slabs)], to_cast)]
    return outs[0], casted, (outs[-1] if zero_rows else None)


def _route_tile(x, nw_ref, r_ref, tri_ref):
    h = _rmsnorm(x, nw_ref[...])
    ne = r_ref.shape[0]
    logit = [jnp.sum(h * r_ref[e:e + 1, :], axis=-1, keepdims=True) for e in range(ne)]
    m1 = logit[0]
    i1 = jnp.zeros(m1.shape, jnp.int32)
    for e in range(1, ne):
        better = logit[e] > m1
        m1 = jnp.where(better, logit[e], m1)
        i1 = jnp.where(better, e, i1)
    m2 = jnp.full(m1.shape, -jnp.inf, F32)
    i2 = jnp.zeros(m1.shape, jnp.int32)
    for e in range(ne):
        cand = jnp.where(i1 == e, -jnp.inf, logit[e])
        better = cand > m2
        m2 = jnp.where(better, cand, m2)
        i2 = jnp.where(better, e, i2)
    e2 = jnp.exp(m2 - m1)
    den = 1.0 + e2
    slot = lax.broadcasted_iota(jnp.int32, (x.shape[0], TOP_K), 1)
    idx = jnp.where(slot == 0, i1, i2)
    wts = jnp.where(slot == 0, 1.0 / den, e2 / den)
    lanes = lax.broadcasted_iota(jnp.int32, (1, LANE), 1)
    chosen = jnp.where(jnp.logical_or(lanes == i1, lanes == i2), 1.0, 0.0).astype(BF16)
    csum = _dot(tri_ref[...], chosen)
    rank1 = jnp.sum(jnp.where(lanes == i1, csum, 0.0), axis=-1, keepdims=True)
    rank2 = jnp.sum(jnp.where(lanes == i2, csum, 0.0), axis=-1, keepdims=True)
    rank = (jnp.where(slot == 0, rank1, rank2) - 1.0).astype(jnp.int32)
    tm = x.shape[0]
    return h.astype(BF16), idx, wts, rank, csum[tm - 1:tm, :].astype(jnp.int32)


def _route(tile_cnt, bm, tc):
    nt, n_experts = tile_cnt.shape
    counts = jnp.sum(tile_cnt, axis=0)
    nb_e = (counts + bm - 1) // bm + 1
    blk_end = jnp.cumsum(nb_e)
    blk_start = blk_end - nb_e
    row0 = blk_start * bm

    nb = (nt * tc * TOP_K) // bm + 2 * n_experts
    bi = jnp.arange(nb, dtype=jnp.int32)
    blk_e = jnp.minimum(jnp.sum((bi[:, None] >= blk_end[None, :]).astype(jnp.int32), axis=1),
                        n_experts - 1)
    blk_valid = jnp.clip(counts[blk_e] - (bi - blk_start[blk_e]) * bm, 0, bm)
    blk_valid = jnp.where(bi < blk_end[-1], blk_valid, 0)

    tile_prev = jnp.cumsum(tile_cnt, axis=0) - tile_cnt
    pair_row = row0[None, :] + tile_prev
    i32 = lambda a: a.astype(jnp.int32)
    return i32(blk_e), i32(blk_valid), i32(pair_row), i32(tile_cnt)


def _dispatch_window_write(wbuf, x_hbm, sem, start, slot, e):
    return pltpu.make_async_copy(wbuf.at[slot, e], x_hbm.at[pl.ds(start, DISPATCH_WIN), :],
                                 sem.at[slot, e])


def _dispatch_kernel(row_ref, cnt_ref, h_ref, idx_ref, rank_ref, zeros_hbm, x_hbm,
                     wbuf, xbuf, carry, sem, xsem):
    del zeros_hbm
    j = pl.program_id(0)
    nt = pl.num_programs(0)
    ne = carry.shape[0]
    win = DISPATCH_WIN
    grp = BF16_SUBLANES
    slot = j % 2

    @pl.when(j == 0)
    def _():
        carry[...] = jnp.zeros_like(carry)

    @pl.when(j > 0)
    def _():
        for e in range(ne):
            _dispatch_window_write(wbuf, x_hbm, sem, 0, 1 - slot, e).wait()

    h = h_ref[...]
    idx = idx_ref[0]
    rank = rank_ref[0]
    rows = lax.broadcasted_iota(jnp.int32, (win, 1), 0)

    def selection(e, lead):
        hit = None
        for kk in range(TOP_K):
            off = jnp.where(idx[kk:kk + 1, :] == e, rank[kk:kk + 1, :] + lead, -1)
            hk = off == rows
            hit = hk if hit is None else jnp.logical_or(hit, hk)
        return jnp.where(hit, 1.0, 0.0).astype(BF16)

    a0s, fills, n_mores = [], [], []
    for e in range(ne):
        row = row_ref[j * ne + e]
        cnt = cnt_ref[j * ne + e]
        a0 = pl.multiple_of((row // grp) * grp, grp)
        fill = row - a0 + cnt
        w = _dot(selection(e, row - a0), h)
        head = w[0:grp, :] + carry[e].astype(F32)
        wbuf[slot, e, 0:grp, :] = head.astype(BF16)
        wbuf[slot, e, grp:, :] = w[grp:, :].astype(BF16)
        _dispatch_window_write(wbuf, x_hbm, sem, a0, slot, e).start()
        g = pl.multiple_of(jnp.minimum((fill // grp) * grp, win - grp), grp)
        carry[e] = wbuf[slot, e, pl.ds(g, grp), :]
        a0s.append(a0)
        fills.append(fill)
        n_mores.append(fill // win)

    @pl.when(functools.reduce(lambda a, b: a + b, n_mores) > 0)
    def _():
        for e in range(ne):
            a0, fill, n_more = a0s[e], fills[e], n_mores[e]

            @pl.when(n_more > 0)
            def _(e=e, a0=a0, fill=fill, n_more=n_more):
                lead0 = row_ref[j * ne + e] - a0

                def more(m, c):
                    st = pl.multiple_of(a0 + (m + 1) * win, grp)
                    xbuf[...] = _dot(selection(e, lead0 - (m + 1) * win), h).astype(BF16)
                    cp = pltpu.make_async_copy(xbuf, x_hbm.at[pl.ds(st, win), :], xsem.at[0])
                    cp.start()
                    cp.wait()
                    return c

                lax.fori_loop(0, n_more, more, 0)
                g = pl.multiple_of(((fill - n_more * win) // grp) * grp, grp)
                carry[e] = xbuf[pl.ds(g, grp), :]

    @pl.when(j == nt - 1)
    def _():
        for e in range(ne):
            _dispatch_window_write(wbuf, x_hbm, sem, 0, slot, e).wait()


def _moe_dispatch(h, top_i, rank, pair_row, pair_cnt, zero_rows, tc):
    t, d = h.shape
    n_rows = zero_rows.shape[0]
    nt = t // tc
    ne = pair_row.shape[1]
    lane_major = lambda a: a.reshape(nt, tc, TOP_K).transpose(0, 2, 1)
    grid_spec = pltpu.PrefetchScalarGridSpec(
        num_scalar_prefetch=2,
        grid=(nt,),
        in_specs=[pl.BlockSpec((tc, d), lambda j, a, b: (j, 0)),
                  pl.BlockSpec((1, TOP_K, tc), lambda j, a, b: (j, 0, 0)),
                  pl.BlockSpec((1, TOP_K, tc), lambda j, a, b: (j, 0, 0)),
                  pl.BlockSpec(memory_space=pl.ANY)],
        out_specs=pl.BlockSpec(memory_space=pl.ANY),
        scratch_shapes=[pltpu.VMEM((2, ne, DISPATCH_WIN, d), BF16),
                        pltpu.VMEM((DISPATCH_WIN, d), BF16),
                        pltpu.VMEM((ne, BF16_SUBLANES, d), BF16),
                        pltpu.SemaphoreType.DMA((2, ne)),
                        pltpu.SemaphoreType.DMA((1,))],
    )
    return pl.pallas_call(
        _dispatch_kernel,
        grid_spec=grid_spec,
        out_shape=jax.ShapeDtypeStruct((n_rows, d), BF16),
        input_output_aliases={5: 0},
        compiler_params=pltpu.CompilerParams(
            dimension_semantics=("arbitrary",), vmem_limit_bytes=VMEM_LIMIT),
        name="moe_dispatch",
    )(pair_row.reshape(-1), pair_cnt.reshape(-1), h, lane_major(top_i), lane_major(rank),
      zero_rows)


def _weight_copies(blk_e_ref, wg_hbm, wu_hbm, wd_hbm, wgb, wub, wdb, sem, blk, col, slot):
    tf = wgb.shape[2]
    e = blk_e_ref[blk]
    c0 = pl.multiple_of(col * tf, LANE)
    return (pltpu.make_async_copy(wg_hbm.at[e, :, pl.ds(c0, tf)], wgb.at[slot], sem.at[0, slot]),
            pltpu.make_async_copy(wu_hbm.at[e, :, pl.ds(c0, tf)], wub.at[slot], sem.at[1, slot]),
            pltpu.make_async_copy(wd_hbm.at[e, pl.ds(c0, tf), :], wdb.at[slot], sem.at[2, slot]))


def _experts_kernel(blk_e_ref, valid_ref, x_ref, wg_hbm, wu_hbm, wd_hbm, o_ref,
                    acc, wgb, wub, wdb, sem):
    i = pl.program_id(0)
    f = pl.program_id(1)
    nb = pl.num_programs(0)
    nf = pl.num_programs(1)
    ring = wgb.shape[0]
    ahead = ring - 1
    step = i * nf + f
    slot = step % ring
    active = valid_ref[i] > 0
    first = f == 0
    final = f == nf - 1

    def fetch(s):
        blk = jnp.minimum(s // nf, nb - 1)

        @pl.when(jnp.logical_and(s < nb * nf, valid_ref[blk] > 0))
        def _():
            for cp in _weight_copies(blk_e_ref, wg_hbm, wu_hbm, wd_hbm, wgb, wub, wdb, sem,
                                     blk, s % nf, s % ring):
                cp.start()

    @pl.when(step == 0)
    def _():
        for s in range(ahead):
            fetch(jnp.int32(s))

    fetch(step + ahead)

    @pl.when(active)
    def _():
        for cp in _weight_copies(blk_e_ref, wg_hbm, wu_hbm, wd_hbm, wgb, wub, wdb, sem,
                                 i, f, slot):
            cp.wait()

    def partial_out():
        xb = x_ref[...]
        gt = _dot(xb, wgb[slot])
        up = _dot(xb, wub[slot])
        a = (gt * _sigmoid(gt) * up).astype(BF16)
        return _dot(a, wdb[slot])

    @pl.when(active & first & final)
    def _():
        o_ref[...] = partial_out().astype(o_ref.dtype)

    @pl.when(active & first & jnp.logical_not(final))
    def _():
        acc[...] = partial_out()

    @pl.when(active & jnp.logical_not(first) & jnp.logical_not(final))
    def _():
        acc[...] += partial_out()

    @pl.when(active & jnp.logical_not(first) & final)
    def _():
        o_ref[...] = (acc[...] + partial_out()).astype(o_ref.dtype)

    @pl.when(jnp.logical_not(active) & final)
    def _():
        o_ref[...] = jnp.zeros_like(o_ref)


def _moe_experts(x_sorted, blk_e, blk_valid, wg, wu, wd, bm):
    rows, d = x_sorted.shape
    nb = rows // bm
    tf = MOE_TF
    ff = wg.shape[2]
    assert ff % tf == 0
    nf = ff // tf

    grid_spec = pltpu.PrefetchScalarGridSpec(
        num_scalar_prefetch=2,
        grid=(nb, nf),
        in_specs=[
            pl.BlockSpec((bm, d), lambda i, f, be, bv: (i, 0)),
            pl.BlockSpec(memory_space=pl.ANY),
            pl.BlockSpec(memory_space=pl.ANY),
            pl.BlockSpec(memory_space=pl.ANY),
        ],
        out_specs=pl.BlockSpec((bm, d), lambda i, f, be, bv: (i, 0)),
        scratch_shapes=[pltpu.VMEM((bm, d), F32),
                        pltpu.VMEM((WEIGHT_BUFFERS, d, tf), BF16),
                        pltpu.VMEM((WEIGHT_BUFFERS, d, tf), BF16),
                        pltpu.VMEM((WEIGHT_BUFFERS, tf, d), BF16),
                        pltpu.SemaphoreType.DMA((3, WEIGHT_BUFFERS))],
    )
    return pl.pallas_call(
        _experts_kernel,
        grid_spec=grid_spec,
        out_shape=jax.ShapeDtypeStruct((rows, d), BF16),
        compiler_params=pltpu.CompilerParams(
            dimension_semantics=("arbitrary", "arbitrary"),
            vmem_limit_bytes=VMEM_LIMIT),
        name="moe_experts",
    )(blk_e, blk_valid, x_sorted, wg, wu, wd)


def _window_start(row, n_rows):
    a0 = (row // BF16_SUBLANES) * BF16_SUBLANES
    return pl.multiple_of(jnp.minimum(a0, n_rows - COMBINE_WIN), BF16_SUBLANES)


def _window_copy(y_hbm, ybuf, sem, start, slot, e):
    return pltpu.make_async_copy(y_hbm.at[pl.ds(start, COMBINE_WIN), :],
                                 ybuf.at[slot, pl.ds(e * COMBINE_WIN, COMBINE_WIN), :],
                                 sem.at[slot, e])


def _combine_kernel(row_ref, cnt_ref, x_ref, idx_ref, w_ref, rank_ref, nw_ref, y_hbm,
                    o_ref, ybuf, xtra, zacc, sem, xsem, *, apply_norm):
    j = pl.program_id(0)
    nt = pl.num_programs(0)
    ne = ybuf.shape[1] // COMBINE_WIN
    n_rows = y_hbm.shape[0]
    slot = j % 2

    def start_tile(tile, s):
        for e in range(ne):
            _window_copy(y_hbm, ybuf, sem, _window_start(row_ref[tile * ne + e], n_rows), s, e).start()

    @pl.when(j == 0)
    def _():
        start_tile(0, 0)

    @pl.when(j + 1 < nt)
    def _():
        start_tile(j + 1, 1 - slot)

    idx = idx_ref[...]
    wts = w_ref[...]
    rank = rank_ref[...]
    lanes = lax.broadcasted_iota(jnp.int32, (1, COMBINE_WIN), 1)

    def weights_tile(e, start, skip):
        m = jnp.zeros((idx.shape[0], COMBINE_WIN), F32)
        lead = row_ref[j * ne + e] - start
        for kk in range(TOP_K):
            off = rank[:, kk:kk + 1] + lead
            valid = jnp.logical_and(idx[:, kk:kk + 1] == e, off >= skip)
            off = jnp.where(valid, off, -1)
            m = jnp.where(off == lanes, wts[:, kk:kk + 1], m)
        return m.astype(BF16)

    def finish(s):
        starts = [_window_start(row_ref[j * ne + e], n_rows) for e in range(ne)]
        for e in range(ne):
            _window_copy(y_hbm, ybuf, sem, starts[e], s, e).wait()
        z = x_ref[...]
        for e in range(ne):
            z = z + _dot(weights_tile(e, starts[e], 0),
                         ybuf[s, e * COMBINE_WIN:(e + 1) * COMBINE_WIN, :])
        n_more = [jnp.maximum((row_ref[j * ne + e] - starts[e] + cnt_ref[j * ne + e]
                               + COMBINE_WIN - 1) // COMBINE_WIN - 1, 0) for e in range(ne)]
        any_more = functools.reduce(lambda a, b: a + b, n_more)

        def emit(zz):
            o_ref[...] = _rmsnorm(zz, nw_ref[...]) if apply_norm else zz

        zacc[...] = z
        emit(z)

        @pl.when(any_more > 0)
        def _():
            for e in range(ne):
                def more(m, carry, e=e, start=starts[e]):
                    want = start + (m + 1) * COMBINE_WIN
                    st = pl.multiple_of(jnp.minimum(want, n_rows - COMBINE_WIN), BF16_SUBLANES)
                    cp = pltpu.make_async_copy(y_hbm.at[pl.ds(st, COMBINE_WIN), :], xtra, xsem.at[0])
                    cp.start()
                    cp.wait()
                    zacc[...] += _dot(weights_tile(e, st, want - st), xtra[...])
                    return carry

                lax.fori_loop(0, n_more[e], more, 0)
            emit(zacc[...])

    finish(slot)


def _moe_combine(x2, y_sorted, top_i, top_w, rank, pair_row, pair_cnt, nw, apply_norm):
    t, d = x2.shape
    tm = MOE_TC
    nt = t // tm
    ne = pair_row.shape[1]
    assert y_sorted.shape[0] >= COMBINE_WIN
    grid_spec = pltpu.PrefetchScalarGridSpec(
        num_scalar_prefetch=2,
        grid=(nt,),
        in_specs=[
            pl.BlockSpec((tm, d), lambda i, a, b: (i, 0)),
            pl.BlockSpec((tm, TOP_K), lambda i, a, b: (i, 0)),
            pl.BlockSpec((tm, TOP_K), lambda i, a, b: (i, 0)),
            pl.BlockSpec((tm, TOP_K), lambda i, a, b: (i, 0)),
            pl.BlockSpec(nw.shape, lambda i, a, b: (0, 0)),
            pl.BlockSpec(memory_space=pl.ANY),
        ],
        out_specs=pl.BlockSpec((tm, d), lambda i, a, b: (i, 0)),
        scratch_shapes=[pltpu.VMEM((2, ne * COMBINE_WIN, d), BF16),
                        pltpu.VMEM((COMBINE_WIN, d), BF16),
                        pltpu.VMEM((tm, d), F32),
                        pltpu.SemaphoreType.DMA((2, ne)),
                        pltpu.SemaphoreType.DMA((1,))],
    )
    return pl.pallas_call(
        functools.partial(_combine_kernel, apply_norm=apply_norm),
        grid_spec=grid_spec,
        out_shape=jax.ShapeDtypeStruct((t, d), F32),
        compiler_params=pltpu.CompilerParams(
            dimension_semantics=("arbitrary",), vmem_limit_bytes=VMEM_LIMIT),
        name="moe_combine",
    )(pair_row.reshape(-1), pair_cnt.reshape(-1), x2, top_i, top_w, rank, nw, y_sorted)


def _final_norm_kernel(x_ref, nw_ref, o_ref):
    o_ref[...] = _rmsnorm(x_ref[...], nw_ref[...])


def _final_norm(x2, nw):
    t, d = x2.shape
    tm = DENSE_TM
    return pl.pallas_call(
        _final_norm_kernel,
        grid=(t // tm,),
        in_specs=[pl.BlockSpec((tm, d), lambda i: (i, 0)), _const_spec(nw.shape)],
        out_specs=pl.BlockSpec((tm, d), lambda i: (i, 0)),
        out_shape=jax.ShapeDtypeStruct((t, d), F32),
        name="final_norm",
    )(x2, nw)


def kernel(x, attn_norm_w, w_in, gate_bias, pool_w, pool_scale, pool_up, hgrn_lower_bounds,
           hgrn_norm_w, hgrn_up, w_out, ffn_norm_w, dense_w_gate, dense_w_up, dense_w_down,
           moe_router, moe_w_gate, moe_w_up, moe_w_down, final_norm_w):
    b, s, d = x.shape
    depth = w_in.shape[0]
    n_experts = moe_router.shape[-1]

    lb_soft = jax.nn.softmax(hgrn_lower_bounds.astype(F32), axis=0)
    lb = jnp.clip(jnp.cumsum(lb_soft, axis=0) - lb_soft[0:1], 0.0, 1.0 - 1e-6)
    lbf = jnp.maximum(lb, LB_FLOOR)
    oml = 1.0 - lb
    dlb = lb - lbf

    def row(a):
        return a.reshape(1, -1).astype(F32)

    final_w = row(final_norm_w)
    n_sorted_rows = (b * s * TOP_K // MOE_BM + 2 * n_experts) * MOE_BM
    normed = None
    for l in range(depth):
        j = l // 2
        last = l == depth - 1
        routed = l % 2 == 1
        mixed = _token_mixer(
            x, row(attn_norm_w[l]), w_in[l].astype(BF16), row(gate_bias[l]),
            pool_w[l].astype(BF16), row(pool_scale[l]), pool_up[l].astype(BF16),
            row(lbf[l]), row(oml[l]), row(dlb[l]),
            row(jnp.tile(hgrn_norm_w[l], HGRN_HEADS)), hgrn_up[l].astype(BF16),
            w_out[l].astype(BF16),
            router=(row(ffn_norm_w[l]), moe_router[j].astype(F32).T) if routed else None)
        if routed:
            x, h, top_i, top_w, rank, tile_cnt = mixed
        else:
            x = mixed
        x2 = x.reshape(b * s, d)
        if l % 2 == 0:
            routed_next = l + 1 < depth
            to_cast = (moe_w_gate[j], moe_w_up[j], moe_w_down[j]) if routed_next else ()
            x2, moe_w16, sorted_zeros = _dense_ffn(
                x2, row(ffn_norm_w[l]), dense_w_gate[j].astype(BF16),
                dense_w_up[j].astype(BF16), dense_w_down[j].astype(BF16),
                to_cast, n_sorted_rows if routed_next else 0)
            if last:
                normed = _final_norm(x2, final_w)
        else:
            blk_e, blk_valid, pair_row, pair_cnt = _route(tile_cnt, MOE_BM, MOE_TC)
            x_sorted = _moe_dispatch(h, top_i, rank, pair_row, pair_cnt, sorted_zeros, MOE_TC)
            y_sorted = _moe_experts(x_sorted, blk_e, blk_valid, *moe_w16, MOE_BM)
            x2 = _moe_combine(x2, y_sorted, top_i, top_w, rank, pair_row, pair_cnt, final_w,
                              apply_norm=last)
            if last:
                normed = x2
        x = x2.reshape(b, s, d)
    return normed.reshape(b, s, d)
```
